```python
import jax, jax.numpy as jnp
from jax import lax
import numpy as np

D_MODEL = 1024
BATCH = 8
SEQ = 2048
DEPTH = 1


N_MEM = 256
GRID_W = 64
HEAD_DIM = 64
ATTN_HEADS = 8
KV_HEADS = 2
ATTN_WIDTH = ATTN_HEADS * HEAD_DIM
KV_WIDTH = KV_HEADS * HEAD_DIM
CONV_WIDTH = D_MODEL - ATTN_WIDTH
CONV_GROUPS = 8
CONV_K = 31
MEM_HEADS = 4
MEM_HEAD_DIM = D_MODEL // MEM_HEADS
D_FF = 4 * D_MODEL
Q_BLOCK = 128
AXIS_DIM = HEAD_DIM // 2
ROPE_THETA = 10000.0
EPS = 1e-5
ALPHA = (2 * DEPTH) ** 0.25
BETA = (8 * DEPTH) ** -0.25
IN_COLS = ATTN_WIDTH + 2 * KV_WIDTH + 2 * CONV_WIDTH

kernel_name = 'hybrid_gqa_conformer_deepnorm_encoder'


def _dense(key, shape, fan_in, scale=1.0):
    return jax.random.normal(key, shape, jnp.float32) * (scale * fan_in ** -0.5)


def _gain(key, shape):
    return 1.0 + 0.02 * jax.random.normal(key, shape, jnp.float32)


def _bias(key, shape):
    return 0.02 * jax.random.normal(key, shape, jnp.float32)


def _layernorm(x, g, b):
    xf = x.astype(jnp.float32)
    mu = jnp.mean(xf, axis=-1, keepdims=True)
    var = jnp.mean(jnp.square(xf - mu), axis=-1, keepdims=True)
    return ((xf - mu) * lax.rsqrt(var + EPS) * g + b).astype(x.dtype)


def _rmsnorm(x, g):
    xf = x.astype(jnp.float32)
    return (xf * lax.rsqrt(jnp.mean(jnp.square(xf), axis=-1, keepdims=True) + EPS) * g).astype(x.dtype)


def _group_rms(y, n_groups, g):
    b, s, w = y.shape
    yf = y.astype(jnp.float32).reshape(b, s, n_groups, w // n_groups)
    yf = yf * lax.rsqrt(jnp.mean(jnp.square(yf), axis=-1, keepdims=True) + EPS)
    return (yf.reshape(b, s, w) * g).astype(y.dtype)


def _axial_rotary(seq_len):
    rows = seq_len // GRID_W
    row_ids = jnp.repeat(jnp.arange(rows, dtype=jnp.int32), GRID_W)
    col_ids = jnp.tile(jnp.arange(GRID_W, dtype=jnp.int32), rows)
    inv = ROPE_THETA ** (-jnp.arange(0, AXIS_DIM, 2, dtype=jnp.float32) / AXIS_DIM)
    ang = jnp.concatenate([row_ids[:, None].astype(jnp.float32) * inv,
                           col_ids[:, None].astype(jnp.float32) * inv], axis=-1)
    return jnp.cos(ang), jnp.sin(ang)


def _rope(x, cos, sin):
    xf = x.astype(jnp.float32).reshape(*x.shape[:-1], HEAD_DIM // 2, 2)
    x1, x2 = xf[..., 0], xf[..., 1]
    c, s = cos[:, None, :], sin[:, None, :]
    out = jnp.stack([x1 * c - x2 * s, x1 * s + x2 * c], axis=-1).reshape(x.shape)
    return out.astype(x.dtype)


def _gqa_blocked(q, k, v):
    b, s = q.shape[0], q.shape[1]
    nb = s // Q_BLOCK
    grp = ATTN_HEADS // KV_HEADS
    qb = q.reshape(b, nb, Q_BLOCK, KV_HEADS, grp, HEAD_DIM).transpose(1, 0, 3, 4, 2, 5)
    scale = HEAD_DIM ** -0.5

    def one_block(q_blk):
        sc = jnp.einsum('bkgqd,bskd->bkgqs', q_blk, k).astype(jnp.float32) * scale
        p = jax.nn.softmax(sc, axis=-1).astype(v.dtype)
        return jnp.einsum('bkgqs,bskd->bkgqd', p, v)

    o = lax.map(one_block, qb)
    return o.transpose(1, 0, 4, 2, 3, 5).reshape(b, s, ATTN_WIDTH)


def _depthwise_conv(u, w, bias):
    y = lax.conv_general_dilated(u, w[:, None, :], window_strides=(1,),
                                 padding=[(CONV_K // 2, CONV_K // 2)],
                                 dimension_numbers=('NWC', 'WIO', 'NWC'),
                                 feature_group_count=CONV_WIDTH)
    return y + bias


def _hybrid_mixer(h, cos, sin, w_in, q_norm_g, k_norm_g, conv_w, conv_b, conv_ln_g, conv_ln_b,
                  attn_out_g, conv_out_g, w_out):
    b, s, _ = h.shape
    proj = h @ w_in
    q, k, v, c = jnp.split(proj, [ATTN_WIDTH, ATTN_WIDTH + KV_WIDTH, ATTN_WIDTH + 2 * KV_WIDTH], axis=-1)
    q = _rope(_rmsnorm(q.reshape(b, s, ATTN_HEADS, HEAD_DIM), q_norm_g), cos, sin)
    k = _rope(_rmsnorm(k.reshape(b, s, KV_HEADS, HEAD_DIM), k_norm_g), cos, sin)
    v = v.reshape(b, s, KV_HEADS, HEAD_DIM)
    attn = _gqa_blocked(q, k, v)
    val, gate = jnp.split(c, 2, axis=-1)
    u = val * jax.nn.sigmoid(gate)
    u = _depthwise_conv(u, conv_w, conv_b)
    u = jax.nn.silu(_layernorm(u, conv_ln_g, conv_ln_b))
    y = jnp.concatenate([_group_rms(attn, ATTN_HEADS, attn_out_g),
                         _group_rms(u, CONV_GROUPS, conv_out_g)], axis=-1)
    return y @ w_out


def _memory_attention(h, mem, w_mem_q, w_mem_kv, w_mem_o):
    b, s, _ = h.shape
    m = mem.shape[1]
    q = (h @ w_mem_q).reshape(b, s, MEM_HEADS, MEM_HEAD_DIM)
    k, v = jnp.split(mem @ w_mem_kv, 2, axis=-1)
    k = k.reshape(b, m, MEM_HEADS, MEM_HEAD_DIM)
    v = v.reshape(b, m, MEM_HEADS, MEM_HEAD_DIM)
    sc = jnp.einsum('bshd,bmhd->bhsm', q, k).astype(jnp.float32) * (MEM_HEAD_DIM ** -0.5)
    p = jax.nn.softmax(sc, axis=-1).astype(v.dtype)
    o = jnp.einsum('bhsm,bmhd->bshd', p, v).reshape(b, s, D_MODEL)
    return o @ w_mem_o


def _sq_relu_mlp(h, w_ff1, b_ff1, w_ff2, b_ff2):
    return jnp.square(jax.nn.relu(h @ w_ff1 + b_ff1)) @ w_ff2 + b_ff2


def setup_inputs(seed: int = 0) -> dict:
    key = jax.random.key(seed)
    ks = jax.random.split(key, 30)
    L, D = DEPTH, D_MODEL
    x = jax.random.normal(ks[0], (BATCH, SEQ, D), jnp.float32)
    mem = jax.random.normal(ks[1], (BATCH, N_MEM, D), jnp.float32)
    w_in = jnp.concatenate([_dense(ks[4], (L, D, ATTN_WIDTH + KV_WIDTH), D),
                            _dense(ks[5], (L, D, KV_WIDTH), D, BETA),
                            _dense(ks[6], (L, D, 2 * CONV_WIDTH), D)], axis=-1)
    w_mem_kv = jnp.concatenate([_dense(ks[19], (L, D, D), D),
                                _dense(ks[20], (L, D, D), D, BETA)], axis=-1)
    return {
        'x': x,
        'mem': mem,
        'ln_in_g': _gain(ks[2], (D,)),
        'ln_in_b': _bias(ks[3], (D,)),
        'w_in': w_in,
        'q_norm_g': _gain(ks[7], (L, HEAD_DIM)),
        'k_norm_g': _gain(ks[8], (L, HEAD_DIM)),
        'conv_w': _dense(ks[9], (L, CONV_K, CONV_WIDTH), CONV_K),
        'conv_b': _bias(ks[10], (L, CONV_WIDTH)),
        'conv_ln_g': _gain(ks[11], (L, CONV_WIDTH)),
        'conv_ln_b': _bias(ks[12], (L, CONV_WIDTH)),
        'attn_out_g': _gain(ks[13], (L, ATTN_WIDTH)),
        'conv_out_g': _gain(ks[14], (L, CONV_WIDTH)),
        'w_out': _dense(ks[15], (L, D, D), D, BETA),
        'ln1_g': _gain(ks[16], (L, D)),
        'ln1_b': _bias(ks[17], (L, D)),
        'w_mem_q': _dense(ks[18], (L, D, D), D),
        'w_mem_kv': w_mem_kv,
        'w_mem_o': _dense(ks[21], (L, D, D), D, BETA),
        'ln2_g': _gain(ks[22], (L, D)),
        'ln2_b': _bias(ks[23], (L, D)),
        'w_ff1': _dense(ks[24], (L, D, D_FF), D, BETA),
        'b_ff1': _bias(ks[25], (L, D_FF)),
        'w_ff2': _dense(ks[26], (L, D_FF, D), D_FF, BETA),
        'b_ff2': _bias(ks[27], (L, D)),
        'ln3_g': _gain(ks[28], (L, D)),
        'ln3_b': _bias(ks[29], (L, D)),
    }


def reference(x, mem, ln_in_g, ln_in_b, w_in, q_norm_g, k_norm_g, conv_w, conv_b, conv_ln_g, conv_ln_b,
              attn_out_g, conv_out_g, w_out, ln1_g, ln1_b, w_mem_q, w_mem_kv, w_mem_o, ln2_g, ln2_b,
              w_ff1, b_ff1, w_ff2, b_ff2, ln3_g, ln3_b):
    cos, sin = _axial_rotary(x.shape[1])
    h = _layernorm(x, ln_in_g, ln_in_b)
    for l in range(DEPTH):
        mix = _hybrid_mixer(h, cos, sin, w_in[l], q_norm_g[l], k_norm_g[l], conv_w[l], conv_b[l],
                            conv_ln_g[l], conv_ln_b[l], attn_out_g[l], conv_out_g[l], w_out[l])
        h = _layernorm(ALPHA * h + mix, ln1_g[l], ln1_b[l])
        h = _layernorm(ALPHA * h + _memory_attention(h, mem, w_mem_q[l], w_mem_kv[l], w_mem_o[l]),
                       ln2_g[l], ln2_b[l])
        h = _layernorm(ALPHA * h + _sq_relu_mlp(h, w_ff1[l], b_ff1[l], w_ff2[l], b_ff2[l]),
                       ln3_g[l], ln3_b[l])
    return h
```

```python
import functools

import jax
import jax.numpy as jnp
from jax import lax
from jax.experimental import pallas as pl
from jax.experimental.pallas import tpu as pltpu

D_MODEL = 1024
HEAD_DIM = 64
ATTN_HEADS = 8
KV_HEADS = 2
ATTN_WIDTH = ATTN_HEADS * HEAD_DIM
KV_WIDTH = KV_HEADS * HEAD_DIM
CONV_WIDTH = D_MODEL - ATTN_WIDTH
CONV_K = 31
CONV_HALO = 16
MEM_HEADS = 4
MEM_HEAD_DIM = D_MODEL // MEM_HEADS
D_FF = 4 * D_MODEL
GRID_W = 64
AXIS_DIM = HEAD_DIM // 2
ROPE_THETA = 10000.0
EPS = 1e-5
DEPTH = 1
ALPHA = (2 * DEPTH) ** 0.25

LANES = 128
VMEM_LIMIT = 56 * 1024 * 1024

F32 = jnp.float32
BF16 = jnp.bfloat16


def _layernorm(z, g, b):
    mu = jnp.mean(z, axis=-1, keepdims=True)
    zc = z - mu
    var = jnp.mean(zc * zc, axis=-1, keepdims=True)
    return zc * lax.rsqrt(var + EPS) * g + b


def _seg_mean_sq(z, bd):
    s = z * z
    hi = s.astype(BF16)
    lo = (s - hi.astype(F32)).astype(BF16)
    tot = (jnp.dot(hi, bd, preferred_element_type=F32)
           + jnp.dot(lo, bd, preferred_element_type=F32))
    return tot * (1.0 / HEAD_DIM)


def _in_proj_kernel(x_ref, g_ref, b_ref, w_ref, cq_ref, sq_ref, ck_ref, sk_ref, bd_ref,
                    h_ref, q_ref, kd_ref, vd_ref, u_ref):
    tm = x_ref.shape[0]
    h = _layernorm(x_ref[...], g_ref[...], b_ref[...])
    h_ref[...] = h
    hb = h.astype(BF16)
    bd = bd_ref[...]
    lane = lax.broadcasted_iota(jnp.int32, (tm, LANES), 1)
    even = (lane & 1) == 0
    lo = lane < HEAD_DIM

    def norm_rope(z, c, s):
        sw = jnp.where(even, pltpu.roll(z, LANES - 1, 1), pltpu.roll(z, 1, 1))
        r = lax.rsqrt(_seg_mean_sq(z, bd) + EPS)
        return r * (z * c + sw * s)

    zq = jnp.dot(hb, w_ref[:, 0:ATTN_WIDTH], preferred_element_type=F32)
    cq = cq_ref[...]
    sq = sq_ref[...]
    for g in range(ATTN_WIDTH // LANES):
        sl = slice(g * LANES, (g + 1) * LANES)
        q_ref[:, sl] = norm_rope(zq[:, sl], cq, sq).astype(BF16)

    zkv = jnp.dot(hb, w_ref[:, ATTN_WIDTH:ATTN_WIDTH + 2 * KV_WIDTH], preferred_element_type=F32)
    kk = norm_rope(zkv[:, 0:LANES], ck_ref[...], sk_ref[...])
    kr = pltpu.roll(kk, HEAD_DIM, 1)
    kd_ref[:, 0:LANES] = jnp.where(lo, kk, kr).astype(BF16)
    kd_ref[:, LANES:2 * LANES] = jnp.where(lo, kr, kk).astype(BF16)
    vv = zkv[:, LANES:2 * LANES]
    vr = pltpu.roll(vv, HEAD_DIM, 1)
    ones = jnp.ones((tm, LANES), BF16)
    vd_ref[:, 0:LANES] = jnp.where(lo, vv, vr).astype(BF16)
    vd_ref[:, LANES:2 * LANES] = ones
    vd_ref[:, 2 * LANES:3 * LANES] = jnp.where(lo, vr, vv).astype(BF16)
    vd_ref[:, 3 * LANES:4 * LANES] = ones

    c0 = ATTN_WIDTH + 2 * KV_WIDTH
    val = jnp.dot(hb, w_ref[:, c0:c0 + CONV_WIDTH], preferred_element_type=F32)
    gate = jnp.dot(hb, w_ref[:, c0 + CONV_WIDTH:c0 + 2 * CONV_WIDTH], preferred_element_type=F32)
    u_ref[...] = val * jax.nn.sigmoid(gate)


def _in_proj(x2, ln_g, ln_b, w_in, cq, sq, ck, sk, bd, *, seq, tm):
    m, d = x2.shape
    n_in = w_in.shape[1]
    ns = seq // tm
    row = lambda i: (i, 0)
    const = lambda i: (0, 0)
    tab = lambda i: (i % ns, 0)
    return pl.pallas_call(
        _in_proj_kernel,
        grid=(m // tm,),
        in_specs=[
            pl.BlockSpec((tm, d), row),
            pl.BlockSpec((1, d), const),
            pl.BlockSpec((1, d), const),
            pl.BlockSpec((d, n_in), const),
            pl.BlockSpec((tm, LANES), tab),
            pl.BlockSpec((tm, LANES), tab),
            pl.BlockSpec((tm, LANES), tab),
            pl.BlockSpec((tm, LANES), tab),
            pl.BlockSpec((LANES, LANES), const),
        ],
        out_specs=[
            pl.BlockSpec((tm, d), row),
            pl.BlockSpec((tm, ATTN_WIDTH), row),
            pl.BlockSpec((tm, 2 * LANES), row),
            pl.BlockSpec((tm, 4 * LANES), row),
            pl.BlockSpec((tm, CONV_WIDTH), row),
        ],
        out_shape=[
            jax.ShapeDtypeStruct((m, d), F32),
            jax.ShapeDtypeStruct((m, ATTN_WIDTH), BF16),
            jax.ShapeDtypeStruct((m, 2 * LANES), BF16),
            jax.ShapeDtypeStruct((m, 4 * LANES), BF16),
            jax.ShapeDtypeStruct((m, CONV_WIDTH), F32),
        ],
        compiler_params=pltpu.CompilerParams(
            dimension_semantics=("arbitrary",), vmem_limit_bytes=VMEM_LIMIT),
        name="in_proj",
    )(x2, ln_g, ln_b, w_in, cq, sq, ck, sk, bd)


def _mem_kv_kernel(m_ref, w_ref, k_ref, v_ref):
    mb = m_ref[...].astype(BF16)
    k_ref[...] = jnp.dot(mb, w_ref[:, 0:D_MODEL], preferred_element_type=F32).astype(BF16)
    v_ref[...] = jnp.dot(mb, w_ref[:, D_MODEL:2 * D_MODEL], preferred_element_type=F32).astype(BF16)


def _mem_kv(mem2, w_kv, *, tm):
    m, d = mem2.shape
    row = lambda i: (i, 0)
    return pl.pallas_call(
        _mem_kv_kernel,
        grid=(m // tm,),
        in_specs=[pl.BlockSpec((tm, d), row), pl.BlockSpec((d, 2 * d), lambda i: (0, 0))],
        out_specs=[pl.BlockSpec((tm, d), row), pl.BlockSpec((tm, d), row)],
        out_shape=[jax.ShapeDtypeStruct((m, d), BF16), jax.ShapeDtypeStruct((m, d), BF16)],
        compiler_params=pltpu.CompilerParams(
            dimension_semantics=("arbitrary",), vmem_limit_bytes=VMEM_LIMIT),
        name="mem_kv",
    )(mem2, w_kv)


def _attention_kernel(q_ref, k_ref, v_ref, g_ref, o_ref):
    tq = q_ref.shape[1]
    q = q_ref[0]
    k = k_ref[0]
    v = v_ref[0]
    lane = lax.broadcasted_iota(jnp.int32, (tq, LANES), 1)
    lo = lane < HEAD_DIM
    zero = jnp.zeros((tq, LANES), q.dtype)
    parts = []
    for g in range(2):
        qg = q[:, g * LANES:(g + 1) * LANES]
        parts.append(jnp.where(lo, qg, zero))
        parts.append(jnp.where(lo, zero, qg))
    lhs = jnp.concatenate(parts, axis=0)
    s = lax.dot_general(lhs, k, (((1,), (1,)), ((), ())), preferred_element_type=F32)
    m = jnp.max(s, axis=-1, keepdims=True)
    p = jnp.exp(s - m).astype(BF16)
    r = jnp.dot(p, v, preferred_element_type=F32)
    o = r[:, 0:LANES] / r[:, LANES:2 * LANES]
    o = o * lax.rsqrt(jnp.mean(o * o, axis=-1, keepdims=True) + EPS)
    for g in range(2):
        og = jnp.where(lo, o[(2 * g) * tq:(2 * g + 1) * tq], o[(2 * g + 1) * tq:(2 * g + 2) * tq])
        sl = slice(g * LANES, (g + 1) * LANES)
        o_ref[0, :, sl] = (og * g_ref[:, sl]).astype(o_ref.dtype)


def _attention(q, kd, vd, gain, *, tq):
    b, s, _ = q.shape
    return pl.pallas_call(
        _attention_kernel,
        grid=(b, KV_HEADS, s // tq),
        in_specs=[
            pl.BlockSpec((1, tq, 2 * LANES), lambda bi, hi, ji: (bi, ji, hi)),
            pl.BlockSpec((1, s, LANES), lambda bi, hi, ji: (bi, 0, hi)),
            pl.BlockSpec((1, s, 2 * LANES), lambda bi, hi, ji: (bi, 0, hi)),
            pl.BlockSpec((1, 2 * LANES), lambda bi, hi, ji: (0, hi)),
        ],
        out_specs=pl.BlockSpec((1, tq, 2 * LANES), lambda bi, hi, ji: (bi, ji, hi)),
        out_shape=jax.ShapeDtypeStruct((b, s, ATTN_WIDTH), BF16),
        compiler_params=pltpu.CompilerParams(
            dimension_semantics=("arbitrary", "arbitrary", "arbitrary"),
            vmem_limit_bytes=VMEM_LIMIT),
        name="attention",
    )(q, kd, vd, gain)


def _mix_out_kernel(u_ref, a_ref, h_ref, cw_ref, cb_ref, lg_ref, lb_ref, cog_ref, bd_ref,
                    wo_ref, g1_ref, b1_ref, o_ref, upad_ref, conv_ref, *, rows_per_chunk):
    ts = a_ref.shape[1]
    s = u_ref.shape[1]
    j = pl.program_id(1)

    @pl.when(j == 0)
    def _():
        zeros = jnp.zeros((CONV_HALO, CONV_WIDTH), F32)
        upad_ref[0:CONV_HALO, :] = zeros
        upad_ref[CONV_HALO + s:2 * CONV_HALO + s, :] = zeros
        upad_ref[CONV_HALO:CONV_HALO + s, :] = u_ref[0]

    rc = rows_per_chunk
    off0 = CONV_HALO - CONV_K // 2

    def chunk(ci, carry):
        base = pl.multiple_of(j * ts + ci * rc, rc)
        win = upad_ref[pl.ds(base, rc + 2 * CONV_HALO), :]
        acc = jnp.zeros((rc, CONV_WIDTH), F32) + cb_ref[...]
        for r in range(8):
            sh = win[r:r + rc + 2 * CONV_HALO - 8, :]
            for a in range((2 * CONV_HALO) // 8):
                t = 8 * a + r - off0
                if 0 <= t < CONV_K:
                    acc = acc + sh[8 * a:8 * a + rc, :] * cw_ref[t:t + 1, :]
        conv_ref[pl.ds(pl.multiple_of(ci * rc, rc), rc), :] = acc
        return carry

    lax.fori_loop(0, ts // rc, chunk, 0)

    c = _layernorm(conv_ref[...], lg_ref[...], lb_ref[...])
    c = c * jax.nn.sigmoid(c)
    bd = bd_ref[...]
    parts = []
    for g in range(CONV_WIDTH // LANES):
        sl = slice(g * LANES, (g + 1) * LANES)
        cg = c[:, sl]
        parts.append((cg * lax.rsqrt(_seg_mean_sq(cg, bd) + EPS) * cog_ref[:, sl]).astype(BF16))
    yc = jnp.concatenate(parts, axis=1)
    mix = (jnp.dot(a_ref[0], wo_ref[0:ATTN_WIDTH, :], preferred_element_type=F32)
           + jnp.dot(yc, wo_ref[ATTN_WIDTH:D_MODEL, :], preferred_element_type=F32))
    o_ref[0] = _layernorm(ALPHA * h_ref[0] + mix, g1_ref[...], b1_ref[...])


def _mix_out(u, attn_n, h, conv_w, conv_b, ln_g, ln_b, out_g, bd, w_out, g1, b1, *, ts):
    b, s, d = h.shape
    tile = lambda bi, ji: (bi, ji, 0)
    const = lambda bi, ji: (0, 0)
    kern = functools.partial(_mix_out_kernel, rows_per_chunk=32)
    return pl.pallas_call(
        kern,
        grid=(b, s // ts),
        in_specs=[
            pl.BlockSpec((1, s, CONV_WIDTH), lambda bi, ji: (bi, 0, 0)),
            pl.BlockSpec((1, ts, ATTN_WIDTH), tile),
            pl.BlockSpec((1, ts, d), tile),
            pl.BlockSpec((CONV_K, CONV_WIDTH), const),
            pl.BlockSpec((1, CONV_WIDTH), const),
            pl.BlockSpec((1, CONV_WIDTH), const),
            pl.BlockSpec((1, CONV_WIDTH), const),
            pl.BlockSpec((1, CONV_WIDTH), const),
            pl.BlockSpec((LANES, LANES), const),
            pl.BlockSpec((d, d), const),
            pl.BlockSpec((1, d), const),
            pl.BlockSpec((1, d), const),
        ],
        out_specs=pl.BlockSpec((1, ts, d), tile),
        out_shape=jax.ShapeDtypeStruct((b, s, d), F32),
        scratch_shapes=[
            pltpu.VMEM((s + 2 * CONV_HALO, CONV_WIDTH), F32),
            pltpu.VMEM((ts, CONV_WIDTH), F32),
        ],
        compiler_params=pltpu.CompilerParams(
            dimension_semantics=("arbitrary", "arbitrary"), vmem_limit_bytes=VMEM_LIMIT),
        name="mix_out",
    )(u, attn_n, h, conv_w, conv_b, ln_g, ln_b, out_g, bd, w_out, g1, b1)


def _mem_attn_kernel(h_ref, wq_ref, k_ref, v_ref, wo_ref, g_ref, b_ref, o_ref):
    h = h_ref[0]
    hb = h.astype(BF16)
    q = jnp.dot(hb, wq_ref[...], preferred_element_type=F32) * (MEM_HEAD_DIM ** -0.5)
    qb = q.astype(BF16)
    outs = []
    for hd in range(MEM_HEADS):
        sl = slice(hd * MEM_HEAD_DIM, (hd + 1) * MEM_HEAD_DIM)
        s = lax.dot_general(qb[:, sl], k_ref[0, :, sl], (((1,), (1,)), ((), ())),
                            preferred_element_type=F32)
        m = jnp.max(s, axis=-1, keepdims=True)
        p = jnp.exp(s - m)
        l = jnp.sum(p, axis=-1, keepdims=True)
        o = jnp.dot(p.astype(BF16), v_ref[0, :, sl], preferred_element_type=F32) / l
        outs.append(o.astype(BF16))
    ob = jnp.concatenate(outs, axis=1)
    att = jnp.dot(ob, wo_ref[...], preferred_element_type=F32)
    o_ref[0] = _layernorm(ALPHA * h + att, g_ref[...], b_ref[...])


def _mem_attn(h1, wq, km, vm, wo, g2, b2, *, ts):
    b, s, d = h1.shape
    nm = km.shape[1]
    tile = lambda bi, ji: (bi, ji, 0)
    const = lambda bi, ji: (0, 0)
    per_b = lambda bi, ji: (bi, 0, 0)
    return pl.pallas_call(
        _mem_attn_kernel,
        grid=(b, s // ts),
        in_specs=[
            pl.BlockSpec((1, ts, d), tile),
            pl.BlockSpec((d, d), const),
            pl.BlockSpec((1, nm, d), per_b),
            pl.BlockSpec((1, nm, d), per_b),
            pl.BlockSpec((d, d), const),
            pl.BlockSpec((1, d), const),
            pl.BlockSpec((1, d), const),
        ],
        out_specs=pl.BlockSpec((1, ts, d), tile),
        out_shape=jax.ShapeDtypeStruct((b, s, d), F32),
        compiler_params=pltpu.CompilerParams(
            dimension_semantics=("arbitrary", "arbitrary"), vmem_limit_bytes=VMEM_LIMIT),
        name="mem_attn",
    )(h1, wq, km, vm, wo, g2, b2)


def _mlp_kernel(h_ref, w1_ref, b1_ref, w2_ref, b2_ref, g_ref, b_ref, o_ref, *, ff_chunk):
    h = h_ref[...]
    hb = h.astype(BF16)
    acc = ALPHA * h + b2_ref[...]
    for f in range(D_FF // ff_chunk):
        sl = slice(f * ff_chunk, (f + 1) * ff_chunk)
        a = jnp.dot(hb, w1_ref[:, sl], preferred_element_type=F32) + b1_ref[:, sl]
        a = jnp.maximum(a, 0.0)
        acc = acc + jnp.dot((a * a).astype(BF16), w2_ref[sl, :], preferred_element_type=F32)
    o_ref[...] = _layernorm(acc, g_ref[...], b_ref[...])


def _mlp(h2, w1, b1, w2, b2, g3, b3, *, tm):
    m, d = h2.shape
    row = lambda i: (i, 0)
    const = lambda i: (0, 0)
    kern = functools.partial(_mlp_kernel, ff_chunk=1024)
    return pl.pallas_call(
        kern,
        grid=(m // tm,),
        in_specs=[
            pl.BlockSpec((tm, d), row),
            pl.BlockSpec((d, D_FF), const),
            pl.BlockSpec((1, D_FF), const),
            pl.BlockSpec((D_FF, d), const),
            pl.BlockSpec((1, d), const),
            pl.BlockSpec((1, d), const),
            pl.BlockSpec((1, d), const),
        ],
        out_specs=pl.BlockSpec((tm, d), row),
        out_shape=jax.ShapeDtypeStruct((m, d), F32),
        compiler_params=pltpu.CompilerParams(
            dimension_semantics=("arbitrary",), vmem_limit_bytes=VMEM_LIMIT),
        name="mlp",
    )(h2, w1, b1, w2, b2, g3, b3)


def _rope_tables(seq_len, gain, scale):
    rows = seq_len // GRID_W
    row_ids = jnp.repeat(jnp.arange(rows, dtype=jnp.int32), GRID_W)
    col_ids = jnp.tile(jnp.arange(GRID_W, dtype=jnp.int32), rows)
    inv = ROPE_THETA ** (-jnp.arange(0, AXIS_DIM, 2, dtype=jnp.float32) / AXIS_DIM)
    ang = jnp.concatenate([row_ids[:, None].astype(jnp.float32) * inv,
                           col_ids[:, None].astype(jnp.float32) * inv], axis=-1)
    cos = jnp.repeat(jnp.cos(ang), 2, axis=-1)
    sin = jnp.repeat(jnp.sin(ang), 2, axis=-1)
    sign = jnp.where(jnp.arange(HEAD_DIM) % 2 == 0, -1.0, 1.0).astype(jnp.float32)
    gain_sw = gain.reshape(HEAD_DIM // 2, 2)[:, ::-1].reshape(HEAD_DIM)
    c = cos * gain * scale
    s = sin * sign * gain_sw * scale
    return jnp.tile(c, (1, LANES // HEAD_DIM)), jnp.tile(s, (1, LANES // HEAD_DIM))


def kernel(x, mem, ln_in_g, ln_in_b, w_in, q_norm_g, k_norm_g, conv_w, conv_b, conv_ln_g, conv_ln_b,
           attn_out_g, conv_out_g, w_out, ln1_g, ln1_b, w_mem_q, w_mem_kv, w_mem_o, ln2_g, ln2_b,
           w_ff1, b_ff1, w_ff2, b_ff2, ln3_g, ln3_b):
    b, s, d = x.shape
    assert d == D_MODEL and w_in.shape[0] == DEPTH == 1
    assert s % GRID_W == 0
    m = b * s
    nm = mem.shape[1]
    row = lambda a: a.reshape(1, -1)

    cq, sq = _rope_tables(s, q_norm_g[0], HEAD_DIM ** -0.5)
    ck, sk = _rope_tables(s, k_norm_g[0], 1.0)
    seg = jnp.arange(LANES) // HEAD_DIM
    bd = (seg[:, None] == seg[None, :]).astype(BF16)

    h, q, kd, vd, u = _in_proj(x.reshape(m, d), row(ln_in_g), row(ln_in_b), w_in[0].astype(BF16),
                               cq, sq, ck, sk, bd, seq=s, tm=512)
    km, vm = _mem_kv(mem.reshape(b * nm, d), w_mem_kv[0].astype(BF16), tm=512)

    attn_n = _attention(q.reshape(b, s, ATTN_WIDTH), kd.reshape(b, s, 2 * LANES),
                        vd.reshape(b, s, 4 * LANES), row(attn_out_g[0]), tq=256)

    h1 = _mix_out(u.reshape(b, s, CONV_WIDTH), attn_n, h.reshape(b, s, d), conv_w[0], row(conv_b[0]),
                  row(conv_ln_g[0]), row(conv_ln_b[0]), row(conv_out_g[0]), bd,
                  w_out[0].astype(BF16), row(ln1_g[0]), row(ln1_b[0]), ts=256)

    h2 = _mem_attn(h1, w_mem_q[0].astype(BF16), km.reshape(b, nm, d), vm.reshape(b, nm, d),
                   w_mem_o[0].astype(BF16), row(ln2_g[0]), row(ln2_b[0]), ts=512)

    out = _mlp(h2.reshape(m, d), w_ff1[0].astype(BF16), row(b_ff1[0]), w_ff2[0].astype(BF16),
               row(b_ff2[0]), row(ln3_g[0]), row(ln3_b[0]), tm=512)
    return out.reshape(b, s, d)
```

```python
import functools

import jax
import jax.numpy as jnp
from jax import lax
from jax.experimental import pallas as pl
from jax.experimental.pallas import tpu as pltpu

D_MODEL = 1024
HEAD_DIM = 64
ATTN_HEADS = 8
KV_HEADS = 2
ATTN_WIDTH = ATTN_HEADS * HEAD_DIM
KV_WIDTH = KV_HEADS * HEAD_DIM
CONV_WIDTH = D_MODEL - ATTN_WIDTH
CONV_K = 31
CONV_HALO = 16
MEM_HEADS = 4
MEM_HEAD_DIM = D_MODEL // MEM_HEADS
D_FF = 4 * D_MODEL
GRID_W = 64
AXIS_DIM = HEAD_DIM // 2
ROPE_THETA = 10000.0
EPS = 1e-5
DEPTH = 1
ALPHA = (2 * DEPTH) ** 0.25
LOG2_E = 1.4426950408889634

LANES = 128
VMEM_LIMIT = 56 * 1024 * 1024

F32 = jnp.float32
BF16 = jnp.bfloat16


def _layernorm(z, g, b):
    mu = jnp.mean(z, axis=-1, keepdims=True)
    zc = z - mu
    var = jnp.mean(zc * zc, axis=-1, keepdims=True)
    return zc * lax.rsqrt(var + EPS) * g + b


def _seg_mean_sq(z, bd):
    s = z * z
    hi = s.astype(BF16)
    lo = (s - hi.astype(F32)).astype(BF16)
    tot = (jnp.dot(hi, bd, preferred_element_type=F32)
           + jnp.dot(lo, bd, preferred_element_type=F32))
    return tot * (1.0 / HEAD_DIM)


def _in_proj_kernel(x_ref, g_ref, b_ref, w_ref, cq_ref, sq_ref, ck_ref, sk_ref, bd_ref,
                    h_ref, q_ref, kd_ref, vd_ref, u_ref, *, sub_rows):
    bd = bd_ref[...]
    lane = lax.broadcasted_iota(jnp.int32, (sub_rows, LANES), 1)
    even = (lane & 1) == 0
    lo = lane < HEAD_DIM
    ones = jnp.ones((sub_rows, LANES), BF16)
    c0 = ATTN_WIDTH + 2 * KV_WIDTH

    def norm_rope(z, c, s):
        sw = jnp.where(even, pltpu.roll(z, LANES - 1, 1), pltpu.roll(z, 1, 1))
        r = lax.rsqrt(_seg_mean_sq(z, bd) + EPS)
        return r * (z * c + sw * s)

    for st in range(x_ref.shape[0] // sub_rows):
        rows = slice(st * sub_rows, (st + 1) * sub_rows)
        h = _layernorm(x_ref[rows, :], g_ref[...], b_ref[...])
        h_ref[rows, :] = h
        hb = h.astype(BF16)

        zq = jnp.dot(hb, w_ref[:, 0:ATTN_WIDTH], preferred_element_type=F32)
        cq = cq_ref[rows, :]
        sq = sq_ref[rows, :]
        for g in range(ATTN_WIDTH // LANES):
            sl = slice(g * LANES, (g + 1) * LANES)
            q_ref[rows, sl] = norm_rope(zq[:, sl], cq, sq).astype(BF16)

        zkv = jnp.dot(hb, w_ref[:, ATTN_WIDTH:c0], preferred_element_type=F32)
        kk = norm_rope(zkv[:, 0:LANES], ck_ref[rows, :], sk_ref[rows, :])
        kr = pltpu.roll(kk, HEAD_DIM, 1)
        kd_ref[rows, 0:LANES] = jnp.where(lo, kk, kr).astype(BF16)
        kd_ref[rows, LANES:2 * LANES] = jnp.where(lo, kr, kk).astype(BF16)
        vv = zkv[:, LANES:2 * LANES]
        vr = pltpu.roll(vv, HEAD_DIM, 1)
        vd_ref[rows, 0:LANES] = jnp.where(lo, vv, vr).astype(BF16)
        vd_ref[rows, LANES:2 * LANES] = ones
        vd_ref[rows, 2 * LANES:3 * LANES] = jnp.where(lo, vr, vv).astype(BF16)
        vd_ref[rows, 3 * LANES:4 * LANES] = ones

        val = jnp.dot(hb, w_ref[:, c0:c0 + CONV_WIDTH], preferred_element_type=F32)
        gate = jnp.dot(hb, w_ref[:, c0 + CONV_WIDTH:c0 + 2 * CONV_WIDTH],
                       preferred_element_type=F32)
        u_ref[rows, :] = val * jax.nn.sigmoid(gate)


def _in_proj(x2, ln_g, ln_b, w_in, cq, sq, ck, sk, bd, *, seq, tm):
    m, d = x2.shape
    n_in = w_in.shape[1]
    ns = seq // tm
    row = lambda i: (i, 0)
    const = lambda i: (0, 0)
    tab = lambda i: (i % ns, 0)
    return pl.pallas_call(
        functools.partial(_in_proj_kernel, sub_rows=256),
        grid=(m // tm,),
        in_specs=[
            pl.BlockSpec((tm, d), row),
            pl.BlockSpec((1, d), const),
            pl.BlockSpec((1, d), const),
            pl.BlockSpec((d, n_in), const),
            pl.BlockSpec((tm, LANES), tab),
            pl.BlockSpec((tm, LANES), tab),
            pl.BlockSpec((tm, LANES), tab),
            pl.BlockSpec((tm, LANES), tab),
            pl.BlockSpec((LANES, LANES), const),
        ],
        out_specs=[
            pl.BlockSpec((tm, d), row),
            pl.BlockSpec((tm, ATTN_WIDTH), row),
            pl.BlockSpec((tm, 2 * LANES), row),
            pl.BlockSpec((tm, 4 * LANES), row),
            pl.BlockSpec((tm, CONV_WIDTH), row),
        ],
        out_shape=[
            jax.ShapeDtypeStruct((m, d), F32),
            jax.ShapeDtypeStruct((m, ATTN_WIDTH), BF16),
            jax.ShapeDtypeStruct((m, 2 * LANES), BF16),
            jax.ShapeDtypeStruct((m, 4 * LANES), BF16),
            jax.ShapeDtypeStruct((m, CONV_WIDTH), F32),
        ],
        compiler_params=pltpu.CompilerParams(
            dimension_semantics=("arbitrary",), vmem_limit_bytes=VMEM_LIMIT),
        name="in_proj",
    )(x2, ln_g, ln_b, w_in, cq, sq, ck, sk, bd)


def _mem_kv_kernel(m_ref, w_ref, k_ref, v_ref):
    mb = m_ref[...].astype(BF16)
    k_ref[...] = jnp.dot(mb, w_ref[:, 0:D_MODEL], preferred_element_type=F32).astype(BF16)
    v_ref[...] = jnp.dot(mb, w_ref[:, D_MODEL:2 * D_MODEL], preferred_element_type=F32).astype(BF16)


def _mem_kv(mem2, w_kv, *, tm):
    m, d = mem2.shape
    row = lambda i: (i, 0)
    return pl.pallas_call(
        _mem_kv_kernel,
        grid=(m // tm,),
        in_specs=[pl.BlockSpec((tm, d), row), pl.BlockSpec((d, 2 * d), lambda i: (0, 0))],
        out_specs=[pl.BlockSpec((tm, d), row), pl.BlockSpec((tm, d), row)],
        out_shape=[jax.ShapeDtypeStruct((m, d), BF16), jax.ShapeDtypeStruct((m, d), BF16)],
        compiler_params=pltpu.CompilerParams(
            dimension_semantics=("arbitrary",), vmem_limit_bytes=VMEM_LIMIT),
        name="mem_kv",
    )(mem2, w_kv)


def _attention_kernel(q_ref, k_ref, v_ref, g_ref, o_ref, *, sub_rows):
    tq = q_ref.shape[1]
    q = q_ref[0]
    k = k_ref[0]
    v = v_ref[0]
    lane = lax.broadcasted_iota(jnp.int32, (sub_rows, LANES), 1)
    lo = lane < HEAD_DIM
    zero = jnp.zeros((sub_rows, LANES), q.dtype)

    def one_head(qg, keep_lo):
        lhs = jnp.where(lo, qg, zero) if keep_lo else jnp.where(lo, zero, qg)
        s = lax.dot_general(lhs, k, (((1,), (1,)), ((), ())), preferred_element_type=F32)
        m = jnp.max(s, axis=-1, keepdims=True)
        p = jnp.exp2(s - m).astype(BF16)
        r = jnp.dot(p, v, preferred_element_type=F32)
        o = r[:, 0:LANES] / r[:, LANES:2 * LANES]
        return o * lax.rsqrt(jnp.mean(o * o, axis=-1, keepdims=True) + EPS)

    for rb in range(tq // sub_rows):
        rows = slice(rb * sub_rows, (rb + 1) * sub_rows)
        for g in range(2):
            sl = slice(g * LANES, (g + 1) * LANES)
            qg = q[rows, sl]
            og = jnp.where(lo, one_head(qg, True), one_head(qg, False))
            o_ref[0, rows, sl] = (og * g_ref[:, sl]).astype(o_ref.dtype)


def _attention(q, kd, vd, gain, *, tq):
    b, s, _ = q.shape
    return pl.pallas_call(
        functools.partial(_attention_kernel, sub_rows=256),
        grid=(b, KV_HEADS, s // tq),
        in_specs=[
            pl.BlockSpec((1, tq, 2 * LANES), lambda bi, hi, ji: (bi, ji, hi)),
            pl.BlockSpec((1, s, LANES), lambda bi, hi, ji: (bi, 0, hi)),
            pl.BlockSpec((1, s, 2 * LANES), lambda bi, hi, ji: (bi, 0, hi)),
            pl.BlockSpec((1, 2 * LANES), lambda bi, hi, ji: (0, hi)),
        ],
        out_specs=pl.BlockSpec((1, tq, 2 * LANES), lambda bi, hi, ji: (bi, ji, hi)),
        out_shape=jax.ShapeDtypeStruct((b, s, ATTN_WIDTH), BF16),
        compiler_params=pltpu.CompilerParams(
            dimension_semantics=("arbitrary", "arbitrary", "arbitrary"),
            vmem_limit_bytes=VMEM_LIMIT),
        name="attention",
    )(q, kd, vd, gain)


def _mix_out_kernel(u_ref, a_ref, h_ref, cw_ref, cb_ref, lg_ref, lb_ref, cog_ref, bd_ref,
                    wo_ref, g1_ref, b1_ref, o_ref, upad_ref, wb_ref, *, rows_per_chunk, sub_rows):
    ts = a_ref.shape[1]
    s = u_ref.shape[1]
    j = pl.program_id(1)
    sub = 8

    @pl.when(j == 0)
    def _():
        zeros = jnp.zeros((CONV_HALO, CONV_WIDTH), F32)
        upad_ref[0:CONV_HALO, :] = zeros
        upad_ref[CONV_HALO + s:2 * CONV_HALO + s, :] = zeros
        upad_ref[CONV_HALO:CONV_HALO + s, :] = u_ref[0]
        for t in range(CONV_K):
            wb_ref[t] = jnp.broadcast_to(cw_ref[t:t + 1, :], (sub, CONV_WIDTH))

    rc = rows_per_chunk
    wrows = rc + 2 * CONV_HALO
    off0 = CONV_HALO - CONV_K // 2
    bias = jnp.broadcast_to(cb_ref[...], (sub, CONV_WIDTH))
    bd = bd_ref[...]

    def conv_chunk(row0):
        win = upad_ref[pl.ds(pl.multiple_of(j * ts + row0, rc), wrows), :]
        acc = [bias] * (rc // sub)
        for r in range(sub):
            sh = win if r == 0 else pltpu.roll(win, wrows - r, 0)
            for a in range((2 * CONV_HALO) // sub):
                t = sub * a + r - off0
                if 0 <= t < CONV_K:
                    w = wb_ref[t]
                    for i in range(rc // sub):
                        lo_row = sub * (a + i)
                        acc[i] = acc[i] + sh[lo_row:lo_row + sub, :] * w
        return acc

    for st in range(ts // sub_rows):
        rows = slice(st * sub_rows, (st + 1) * sub_rows)
        pieces = []
        for ci in range(sub_rows // rc):
            pieces.extend(conv_chunk(st * sub_rows + ci * rc))
        c = _layernorm(jnp.concatenate(pieces, axis=0), lg_ref[...], lb_ref[...])
        c = c * jax.nn.sigmoid(c)
        parts = []
        for g in range(CONV_WIDTH // LANES):
            sl = slice(g * LANES, (g + 1) * LANES)
            cg = c[:, sl]
            parts.append((cg * lax.rsqrt(_seg_mean_sq(cg, bd) + EPS) * cog_ref[:, sl]).astype(BF16))
        yc = jnp.concatenate(parts, axis=1)
        mix = (jnp.dot(a_ref[0, rows, :], wo_ref[0:ATTN_WIDTH, :], preferred_element_type=F32)
               + jnp.dot(yc, wo_ref[ATTN_WIDTH:D_MODEL, :], preferred_element_type=F32))
        o_ref[0, rows, :] = _layernorm(ALPHA * h_ref[0, rows, :] + mix, g1_ref[...], b1_ref[...])


def _mix_out(u, attn_n, h, conv_w, conv_b, ln_g, ln_b, out_g, bd, w_out, g1, b1, *, ts):
    b, s, d = h.shape
    tile = lambda bi, ji: (bi, ji, 0)
    const = lambda bi, ji: (0, 0)
    kern = functools.partial(_mix_out_kernel, rows_per_chunk=32, sub_rows=128)
    return pl.pallas_call(
        kern,
        grid=(b, s // ts),
        in_specs=[
            pl.BlockSpec((1, s, CONV_WIDTH), lambda bi, ji: (bi, 0, 0)),
            pl.BlockSpec((1, ts, ATTN_WIDTH), tile),
            pl.BlockSpec((1, ts, d), tile),
            pl.BlockSpec((CONV_K, CONV_WIDTH), const),
            pl.BlockSpec((1, CONV_WIDTH), const),
            pl.BlockSpec((1, CONV_WIDTH), const),
            pl.BlockSpec((1, CONV_WIDTH), const),
            pl.BlockSpec((1, CONV_WIDTH), const),
            pl.BlockSpec((LANES, LANES), const),
            pl.BlockSpec((d, d), const),
            pl.BlockSpec((1, d), const),
            pl.BlockSpec((1, d), const),
        ],
        out_specs=pl.BlockSpec((1, ts, d), tile),
        out_shape=jax.ShapeDtypeStruct((b, s, d), F32),
        scratch_shapes=[
            pltpu.VMEM((s + 2 * CONV_HALO, CONV_WIDTH), F32),
            pltpu.VMEM((CONV_K, 8, CONV_WIDTH), F32),
        ],
        compiler_params=pltpu.CompilerParams(
            dimension_semantics=("arbitrary", "arbitrary"), vmem_limit_bytes=VMEM_LIMIT),
        name="mix_out",
    )(u, attn_n, h, conv_w, conv_b, ln_g, ln_b, out_g, bd, w_out, g1, b1)


def _mem_attn_kernel(h_ref, wq_ref, k_ref, v_ref, wo_ref, g_ref, b_ref, o_ref, *, sub_rows):
    for st in range(h_ref.shape[1] // sub_rows):
        rows = slice(st * sub_rows, (st + 1) * sub_rows)
        h = h_ref[0, rows, :]
        hb = h.astype(BF16)
        q = jnp.dot(hb, wq_ref[...], preferred_element_type=F32) * (MEM_HEAD_DIM ** -0.5)
        qb = q.astype(BF16)
        outs = []
        for hd in range(MEM_HEADS):
            sl = slice(hd * MEM_HEAD_DIM, (hd + 1) * MEM_HEAD_DIM)
            s = lax.dot_general(qb[:, sl], k_ref[0, :, sl], (((1,), (1,)), ((), ())),
                                preferred_element_type=F32)
            m = jnp.max(s, axis=-1, keepdims=True)
            p = jnp.exp(s - m)
            l = jnp.sum(p, axis=-1, keepdims=True)
            o = jnp.dot(p.astype(BF16), v_ref[0, :, sl], preferred_element_type=F32) / l
            outs.append(o.astype(BF16))
        ob = jnp.concatenate(outs, axis=1)
        att = jnp.dot(ob, wo_ref[...], preferred_element_type=F32)
        o_ref[0, rows, :] = _layernorm(ALPHA * h + att, g_ref[...], b_ref[...])


def _mem_attn(h1, wq, km, vm, wo, g2, b2, *, ts):
    b, s, d = h1.shape
    nm = km.shape[1]
    tile = lambda bi, ji: (bi, ji, 0)
    const = lambda bi, ji: (0, 0)
    per_b = lambda bi, ji: (bi, 0, 0)
    return pl.pallas_call(
        functools.partial(_mem_attn_kernel, sub_rows=256),
        grid=(b, s // ts),
        in_specs=[
            pl.BlockSpec((1, ts, d), tile),
            pl.BlockSpec((d, d), const),
            pl.BlockSpec((1, nm, d), per_b),
            pl.BlockSpec((1, nm, d), per_b),
            pl.BlockSpec((d, d), const),
            pl.BlockSpec((1, d), const),
            pl.BlockSpec((1, d), const),
        ],
        out_specs=pl.BlockSpec((1, ts, d), tile),
        out_shape=jax.ShapeDtypeStruct((b, s, d), F32),
        compiler_params=pltpu.CompilerParams(
            dimension_semantics=("arbitrary", "arbitrary"), vmem_limit_bytes=VMEM_LIMIT),
        name="mem_attn",
    )(h1, wq, km, vm, wo, g2, b2)


def _mlp_kernel(h_ref, w1_ref, b1_ref, w2_ref, b2_ref, g_ref, b_ref, o_ref, *, ff_chunk, sub_rows):
    for st in range(h_ref.shape[0] // sub_rows):
        rows = slice(st * sub_rows, (st + 1) * sub_rows)
        h = h_ref[rows, :]
        hb = h.astype(BF16)
        acc = ALPHA * h + b2_ref[...]
        for f in range(D_FF // ff_chunk):
            sl = slice(f * ff_chunk, (f + 1) * ff_chunk)
            a = jnp.dot(hb, w1_ref[:, sl], preferred_element_type=F32) + b1_ref[:, sl]
            a = jnp.maximum(a, 0.0)
            acc = acc + jnp.dot((a * a).astype(BF16), w2_ref[sl, :], preferred_element_type=F32)
        o_ref[rows, :] = _layernorm(acc, g_ref[...], b_ref[...])


def _mlp(h2, w1, b1, w2, b2, g3, b3, *, tm):
    m, d = h2.shape
    row = lambda i: (i, 0)
    const = lambda i: (0, 0)
    kern = functools.partial(_mlp_kernel, ff_chunk=1024, sub_rows=256)
    return pl.pallas_call(
        kern,
        grid=(m // tm,),
        in_specs=[
            pl.BlockSpec((tm, d), row),
            pl.BlockSpec((d, D_FF), const),
            pl.BlockSpec((1, D_FF), const),
            pl.BlockSpec((D_FF, d), const),
            pl.BlockSpec((1, d), const),
            pl.BlockSpec((1, d), const),
            pl.BlockSpec((1, d), const),
        ],
        out_specs=pl.BlockSpec((tm, d), row),
        out_shape=jax.ShapeDtypeStruct((m, d), F32),
        compiler_params=pltpu.CompilerParams(
            dimension_semantics=("arbitrary",), vmem_limit_bytes=VMEM_LIMIT),
        name="mlp",
    )(h2, w1, b1, w2, b2, g3, b3)


def _rope_tables(seq_len, gain, scale):
    rows = seq_len // GRID_W
    row_ids = jnp.repeat(jnp.arange(rows, dtype=jnp.int32), GRID_W)
    col_ids = jnp.tile(jnp.arange(GRID_W, dtype=jnp.int32), rows)
    inv = ROPE_THETA ** (-jnp.arange(0, AXIS_DIM, 2, dtype=jnp.float32) / AXIS_DIM)
    ang = jnp.concatenate([row_ids[:, None].astype(jnp.float32) * inv,
                           col_ids[:, None].astype(jnp.float32) * inv], axis=-1)
    cos = jnp.repeat(jnp.cos(ang), 2, axis=-1)
    sin = jnp.repeat(jnp.sin(ang), 2, axis=-1)
    sign = jnp.where(jnp.arange(HEAD_DIM) % 2 == 0, -1.0, 1.0).astype(jnp.float32)
    gain_sw = gain.reshape(HEAD_DIM // 2, 2)[:, ::-1].reshape(HEAD_DIM)
    c = cos * gain * scale
    s = sin * sign * gain_sw * scale
    return jnp.tile(c, (1, LANES // HEAD_DIM)), jnp.tile(s, (1, LANES // HEAD_DIM))


def kernel(x, mem, ln_in_g, ln_in_b, w_in, q_norm_g, k_norm_g, conv_w, conv_b, conv_ln_g, conv_ln_b,
           attn_out_g, conv_out_g, w_out, ln1_g, ln1_b, w_mem_q, w_mem_kv, w_mem_o, ln2_g, ln2_b,
           w_ff1, b_ff1, w_ff2, b_ff2, ln3_g, ln3_b):
    b, s, d = x.shape
    assert d == D_MODEL and w_in.shape[0] == DEPTH == 1
    assert s % GRID_W == 0
    m = b * s
    nm = mem.shape[1]
    row = lambda a: a.reshape(1, -1)

    cq, sq = _rope_tables(s, q_norm_g[0], HEAD_DIM ** -0.5 * LOG2_E)
    ck, sk = _rope_tables(s, k_norm_g[0], 1.0)
    seg = jnp.arange(LANES) // HEAD_DIM
    bd = (seg[:, None] == seg[None, :]).astype(BF16)

    h, q, kd, vd, u = _in_proj(x.reshape(m, d), row(ln_in_g), row(ln_in_b), w_in[0].astype(BF16),
                               cq, sq, ck, sk, bd, seq=s, tm=512)
    km, vm = _mem_kv(mem.reshape(b * nm, d), w_mem_kv[0].astype(BF16), tm=512)

    attn_n = _attention(q.reshape(b, s, ATTN_WIDTH), kd.reshape(b, s, 2 * LANES),
                        vd.reshape(b, s, 4 * LANES), row(attn_out_g[0]), tq=1024)

    h1 = _mix_out(u.reshape(b, s, CONV_WIDTH), attn_n, h.reshape(b, s, d), conv_w[0], row(conv_b[0]),
                  row(conv_ln_g[0]), row(conv_ln_b[0]), row(conv_out_g[0]), bd,
                  w_out[0].astype(BF16), row(ln1_g[0]), row(ln1_b[0]), ts=256)

    h2 = _mem_attn(h1, w_mem_q[0].astype(BF16), km.reshape(b, nm, d), vm.reshape(b, nm, d),
                   w_mem_o[0].astype(BF16), row(ln2_g[0]), row(ln2_b[0]), ts=512)

    out = _mlp(h2.reshape(m, d), w_ff1[0].astype(BF16), row(b_ff1[0]), w_ff2[0].astype(BF16),
               row(b_ff2[0]), row(ln3_g[0]), row(ln3_b[0]), tm=512)
    return out.reshape(b, s, d)
```

```python
import functools

import jax
import jax.numpy as jnp
from jax import lax
from jax.experimental import pallas as pl
from jax.experimental.pallas import tpu as pltpu

D_MODEL = 1024
HEAD_DIM = 64
ATTN_HEADS = 8
KV_HEADS = 2
ATTN_WIDTH = ATTN_HEADS * HEAD_DIM
KV_WIDTH = KV_HEADS * HEAD_DIM
CONV_WIDTH = D_MODEL - ATTN_WIDTH
CONV_K = 31
CONV_HALO = 16
MEM_HEADS = 4
MEM_HEAD_DIM = D_MODEL // MEM_HEADS
D_FF = 4 * D_MODEL
GRID_W = 64
AXIS_DIM = HEAD_DIM // 2
ROPE_THETA = 10000.0
EPS = 1e-5
DEPTH = 1
ALPHA = (2 * DEPTH) ** 0.25
LOG2_E = 1.4426950408889634

LANES = 128
VMEM_LIMIT = 56 * 1024 * 1024

IN_PROJ_ROWS = 1024
MEM_KV_ROWS = 512
ATTN_Q_ROWS = 1024
MIX_ROWS = 512
MEM_ATTN_ROWS = 1024
MLP_ROWS = 1024
SUB_ROWS = 256
MIX_SUB_ROWS = 128
CONV_CHUNK_ROWS = 32

F32 = jnp.float32
BF16 = jnp.bfloat16


def _layernorm(z, g, b):
    mu = jnp.mean(z, axis=-1, keepdims=True)
    zc = z - mu
    var = jnp.mean(zc * zc, axis=-1, keepdims=True)
    return zc * lax.rsqrt(var + EPS) * g + b


def _seg_mean_sq(z, bd):
    s = z * z
    hi = s.astype(BF16)
    lo = (s - hi.astype(F32)).astype(BF16)
    tot = (jnp.dot(hi, bd, preferred_element_type=F32)
           + jnp.dot(lo, bd, preferred_element_type=F32))
    return tot * (1.0 / HEAD_DIM)


def _in_proj_kernel(x_ref, g_ref, b_ref, w_ref, cq_ref, sq_ref, ck_ref, sk_ref, bd_ref,
                    h_ref, q_ref, kd_ref, vd_ref, u_ref, *, sub_rows):
    bd = bd_ref[...]
    lane = lax.broadcasted_iota(jnp.int32, (sub_rows, LANES), 1)
    even = (lane & 1) == 0
    lo = lane < HEAD_DIM
    ones = jnp.ones((sub_rows, LANES), BF16)
    c0 = ATTN_WIDTH + 2 * KV_WIDTH

    def norm_rope(z, c, s):
        sw = jnp.where(even, pltpu.roll(z, LANES - 1, 1), pltpu.roll(z, 1, 1))
        r = lax.rsqrt(_seg_mean_sq(z, bd) + EPS)
        return r * (z * c + sw * s)

    subs = [slice(st * sub_rows, (st + 1) * sub_rows) for st in range(x_ref.shape[0] // sub_rows)]
    hbs = []
    for rows in subs:
        h = _layernorm(x_ref[rows, :], g_ref[...], b_ref[...])
        h_ref[rows, :] = h
        hbs.append(h.astype(BF16))
    projs = []
    for hb in hbs:
        projs.append((
            jnp.dot(hb, w_ref[:, 0:ATTN_WIDTH], preferred_element_type=F32),
            jnp.dot(hb, w_ref[:, ATTN_WIDTH:c0], preferred_element_type=F32),
            jnp.dot(hb, w_ref[:, c0:c0 + CONV_WIDTH], preferred_element_type=F32),
            jnp.dot(hb, w_ref[:, c0 + CONV_WIDTH:c0 + 2 * CONV_WIDTH], preferred_element_type=F32)))
    for rows, (zq, zkv, val, gate) in zip(subs, projs):
        u_ref[rows, :] = val * jax.nn.sigmoid(gate)
        cq = cq_ref[rows, :]
        sq = sq_ref[rows, :]
        for g in range(ATTN_WIDTH // LANES):
            sl = slice(g * LANES, (g + 1) * LANES)
            q_ref[rows, sl] = norm_rope(zq[:, sl], cq, sq).astype(BF16)
        kk = norm_rope(zkv[:, 0:LANES], ck_ref[rows, :], sk_ref[rows, :])
        kr = pltpu.roll(kk, HEAD_DIM, 1)
        kd_ref[rows, 0:LANES] = jnp.where(lo, kk, kr).astype(BF16)
        kd_ref[rows, LANES:2 * LANES] = jnp.where(lo, kr, kk).astype(BF16)
        vv = zkv[:, LANES:2 * LANES]
        vr = pltpu.roll(vv, HEAD_DIM, 1)
        vd_ref[rows, 0:LANES] = jnp.where(lo, vv, vr).astype(BF16)
        vd_ref[rows, LANES:2 * LANES] = ones
        vd_ref[rows, 2 * LANES:3 * LANES] = jnp.where(lo, vr, vv).astype(BF16)
        vd_ref[rows, 3 * LANES:4 * LANES] = ones


def _in_proj(x2, ln_g, ln_b, w_in, cq, sq, ck, sk, bd, *, seq, tm):
    m, d = x2.shape
    n_in = w_in.shape[1]
    ns = seq // tm
    row = lambda i: (i, 0)
    const = lambda i: (0, 0)
    tab = lambda i: (i % ns, 0)
    return pl.pallas_call(
        functools.partial(_in_proj_kernel, sub_rows=SUB_ROWS),
        grid=(m // tm,),
        in_specs=[
            pl.BlockSpec((tm, d), row),
            pl.BlockSpec((1, d), const),
            pl.BlockSpec((1, d), const),
            pl.BlockSpec((d, n_in), const, pipeline_mode=pl.Buffered(1)),
            pl.BlockSpec((tm, LANES), tab),
            pl.BlockSpec((tm, LANES), tab),
            pl.BlockSpec((tm, LANES), tab),
            pl.BlockSpec((tm, LANES), tab),
            pl.BlockSpec((LANES, LANES), const),
        ],
        out_specs=[
            pl.BlockSpec((tm, d), row),
            pl.BlockSpec((tm, ATTN_WIDTH), row),
            pl.BlockSpec((tm, 2 * LANES), row),
            pl.BlockSpec((tm, 4 * LANES), row),
            pl.BlockSpec((tm, CONV_WIDTH), row),
        ],
        out_shape=[
            jax.ShapeDtypeStruct((m, d), F32),
            jax.ShapeDtypeStruct((m, ATTN_WIDTH), BF16),
            jax.ShapeDtypeStruct((m, 2 * LANES), BF16),
            jax.ShapeDtypeStruct((m, 4 * LANES), BF16),
            jax.ShapeDtypeStruct((m, CONV_WIDTH), F32),
        ],
        compiler_params=pltpu.CompilerParams(
            dimension_semantics=("arbitrary",), vmem_limit_bytes=VMEM_LIMIT),
        name="in_proj",
    )(x2, ln_g, ln_b, w_in, cq, sq, ck, sk, bd)


def _mem_kv_kernel(m_ref, w_ref, k_ref, v_ref):
    mb = m_ref[...].astype(BF16)
    k_ref[...] = jnp.dot(mb, w_ref[:, 0:D_MODEL], preferred_element_type=F32).astype(BF16)
    v_ref[...] = jnp.dot(mb, w_ref[:, D_MODEL:2 * D_MODEL], preferred_element_type=F32).astype(BF16)


def _mem_kv(mem2, w_kv, *, tm):
    m, d = mem2.shape
    row = lambda i: (i, 0)
    return pl.pallas_call(
        _mem_kv_kernel,
        grid=(m // tm,),
        in_specs=[pl.BlockSpec((tm, d), row), pl.BlockSpec((d, 2 * d), lambda i: (0, 0))],
        out_specs=[pl.BlockSpec((tm, d), row), pl.BlockSpec((tm, d), row)],
        out_shape=[jax.ShapeDtypeStruct((m, d), BF16), jax.ShapeDtypeStruct((m, d), BF16)],
        compiler_params=pltpu.CompilerParams(
            dimension_semantics=("arbitrary",), vmem_limit_bytes=VMEM_LIMIT),
        name="mem_kv",
    )(mem2, w_kv)


def _attention_kernel(q_ref, k_ref, v_ref, g_ref, o_ref, *, sub_rows):
    tq = q_ref.shape[1]
    q = q_ref[0]
    k = k_ref[0]
    v = v_ref[0]
    lane = lax.broadcasted_iota(jnp.int32, (sub_rows, LANES), 1)
    lo = lane < HEAD_DIM
    zero = jnp.zeros((sub_rows, LANES), q.dtype)

    def one_head(qg, keep_lo):
        lhs = jnp.where(lo, qg, zero) if keep_lo else jnp.where(lo, zero, qg)
        s = lax.dot_general(lhs, k, (((1,), (1,)), ((), ())), preferred_element_type=F32)
        m = jnp.max(s, axis=-1, keepdims=True)
        p = jnp.exp2(s - m).astype(BF16)
        r = jnp.dot(p, v, preferred_element_type=F32)
        o = r[:, 0:LANES] / r[:, LANES:2 * LANES]
        return o * lax.rsqrt(jnp.mean(o * o, axis=-1, keepdims=True) + EPS)

    for rb in range(tq // sub_rows):
        rows = slice(rb * sub_rows, (rb + 1) * sub_rows)
        for g in range(2):
            sl = slice(g * LANES, (g + 1) * LANES)
            qg = q[rows, sl]
            og = jnp.where(lo, one_head(qg, True), one_head(qg, False))
            o_ref[0, rows, sl] = (og * g_ref[:, sl]).astype(o_ref.dtype)


def _attention(q, kd, vd, gain, *, tq):
    b, s, _ = q.shape
    return pl.pallas_call(
        functools.partial(_attention_kernel, sub_rows=SUB_ROWS),
        grid=(b, KV_HEADS, s // tq),
        in_specs=[
            pl.BlockSpec((1, tq, 2 * LANES), lambda bi, hi, ji: (bi, ji, hi)),
            pl.BlockSpec((1, s, LANES), lambda bi, hi, ji: (bi, 0, hi)),
            pl.BlockSpec((1, s, 2 * LANES), lambda bi, hi, ji: (bi, 0, hi)),
            pl.BlockSpec((1, 2 * LANES), lambda bi, hi, ji: (0, hi)),
        ],
        out_specs=pl.BlockSpec((1, tq, 2 * LANES), lambda bi, hi, ji: (bi, ji, hi)),
        out_shape=jax.ShapeDtypeStruct((b, s, ATTN_WIDTH), BF16),
        compiler_params=pltpu.CompilerParams(
            dimension_semantics=("arbitrary", "arbitrary", "arbitrary"),
            vmem_limit_bytes=VMEM_LIMIT),
        name="attention",
    )(q, kd, vd, gain)


def _mix_out_kernel(u_ref, a_ref, h_ref, cw_ref, cb_ref, lg_ref, lb_ref, cog_ref, bd_ref,
                    wo_ref, g1_ref, b1_ref, o_ref, upad_ref, wb_ref, *, rows_per_chunk, sub_rows):
    ts = a_ref.shape[1]
    s = u_ref.shape[1]
    j = pl.program_id(1)
    sub = 8

    @pl.when(j == 0)
    def _():
        zeros = jnp.zeros((CONV_HALO, CONV_WIDTH), F32)
        upad_ref[0:CONV_HALO, :] = zeros
        upad_ref[CONV_HALO + s:2 * CONV_HALO + s, :] = zeros
        upad_ref[CONV_HALO:CONV_HALO + s, :] = u_ref[0]
        for t in range(CONV_K):
            wb_ref[t] = jnp.broadcast_to(cw_ref[t:t + 1, :], (sub, CONV_WIDTH))

    rc = rows_per_chunk
    wrows = rc + 2 * CONV_HALO
    off0 = CONV_HALO - CONV_K // 2
    bias = jnp.broadcast_to(cb_ref[...], (sub, CONV_WIDTH))
    bd = bd_ref[...]

    def conv_chunk(row0):
        win = upad_ref[pl.ds(pl.multiple_of(j * ts + row0, rc), wrows), :]
        acc = [bias] * (rc // sub)
        for r in range(sub):
            sh = win if r == 0 else pltpu.roll(win, wrows - r, 0)
            for a in range((2 * CONV_HALO) // sub):
                t = sub * a + r - off0
                if 0 <= t < CONV_K:
                    w = wb_ref[t]
                    for i in range(rc // sub):
                        lo_row = sub * (a + i)
                        acc[i] = acc[i] + sh[lo_row:lo_row + sub, :] * w
        return acc

    for st in range(ts // sub_rows):
        rows = slice(st * sub_rows, (st + 1) * sub_rows)
        pieces = []
        for ci in range(sub_rows // rc):
            pieces.extend(conv_chunk(st * sub_rows + ci * rc))
        c = _layernorm(jnp.concatenate(pieces, axis=0), lg_ref[...], lb_ref[...])
        c = c * jax.nn.sigmoid(c)
        parts = []
        for g in range(CONV_WIDTH // LANES):
            sl = slice(g * LANES, (g + 1) * LANES)
            cg = c[:, sl]
            parts.append((cg * lax.rsqrt(_seg_mean_sq(cg, bd) + EPS) * cog_ref[:, sl]).astype(BF16))
        yc = jnp.concatenate(parts, axis=1)
        mix = (jnp.dot(a_ref[0, rows, :], wo_ref[0:ATTN_WIDTH, :], preferred_element_type=F32)
               + jnp.dot(yc, wo_ref[ATTN_WIDTH:D_MODEL, :], preferred_element_type=F32))
        o_ref[0, rows, :] = _layernorm(ALPHA * h_ref[0, rows, :] + mix, g1_ref[...], b1_ref[...])


def _mix_out(u, attn_n, h, conv_w, conv_b, ln_g, ln_b, out_g, bd, w_out, g1, b1, *, ts):
    b, s, d = h.shape
    tile = lambda bi, ji: (bi, ji, 0)
    const = lambda bi, ji: (0, 0)
    kern = functools.partial(_mix_out_kernel, rows_per_chunk=CONV_CHUNK_ROWS,
                             sub_rows=MIX_SUB_ROWS)
    return pl.pallas_call(
        kern,
        grid=(b, s // ts),
        in_specs=[
            pl.BlockSpec((1, s, CONV_WIDTH), lambda bi, ji: (bi, 0, 0)),
            pl.BlockSpec((1, ts, ATTN_WIDTH), tile),
            pl.BlockSpec((1, ts, d), tile),
            pl.BlockSpec((CONV_K, CONV_WIDTH), const),
            pl.BlockSpec((1, CONV_WIDTH), const),
            pl.BlockSpec((1, CONV_WIDTH), const),
            pl.BlockSpec((1, CONV_WIDTH), const),
            pl.BlockSpec((1, CONV_WIDTH), const),
            pl.BlockSpec((LANES, LANES), const),
            pl.BlockSpec((d, d), const, pipeline_mode=pl.Buffered(1)),
            pl.BlockSpec((1, d), const),
            pl.BlockSpec((1, d), const),
        ],
        out_specs=pl.BlockSpec((1, ts, d), tile),
        out_shape=jax.ShapeDtypeStruct((b, s, d), F32),
        scratch_shapes=[
            pltpu.VMEM((s + 2 * CONV_HALO, CONV_WIDTH), F32),
            pltpu.VMEM((CONV_K, 8, CONV_WIDTH), F32),
        ],
        compiler_params=pltpu.CompilerParams(
            dimension_semantics=("arbitrary", "arbitrary"), vmem_limit_bytes=VMEM_LIMIT),
        name="mix_out",
    )(u, attn_n, h, conv_w, conv_b, ln_g, ln_b, out_g, bd, w_out, g1, b1)


def _mem_attn_kernel(h_ref, wq_ref, k_ref, v_ref, wo_ref, g_ref, b_ref, o_ref, *, sub_rows):
    subs = [slice(st * sub_rows, (st + 1) * sub_rows) for st in range(h_ref.shape[1] // sub_rows)]
    heads = [slice(hd * MEM_HEAD_DIM, (hd + 1) * MEM_HEAD_DIM) for hd in range(MEM_HEADS)]
    scores = []
    for rows in subs:
        hb = h_ref[0, rows, :].astype(BF16)
        q = jnp.dot(hb, wq_ref[...], preferred_element_type=F32) * (MEM_HEAD_DIM ** -0.5)
        qb = q.astype(BF16)
        scores.append([lax.dot_general(qb[:, sl], k_ref[0, :, sl], (((1,), (1,)), ((), ())),
                                       preferred_element_type=F32) for sl in heads])
    atts = []
    for sc in scores:
        outs = []
        for s, sl in zip(sc, heads):
            m = jnp.max(s, axis=-1, keepdims=True)
            p = jnp.exp(s - m)
            l = jnp.sum(p, axis=-1, keepdims=True)
            o = jnp.dot(p.astype(BF16), v_ref[0, :, sl], preferred_element_type=F32) / l
            outs.append(o.astype(BF16))
        atts.append(jnp.dot(jnp.concatenate(outs, axis=1), wo_ref[...], preferred_element_type=F32))
    for rows, att in zip(subs, atts):
        o_ref[0, rows, :] = _layernorm(ALPHA * h_ref[0, rows, :] + att, g_ref[...], b_ref[...])


def _mem_attn(h1, wq, km, vm, wo, g2, b2, *, ts):
    b, s, d = h1.shape
    nm = km.shape[1]
    tile = lambda bi, ji: (bi, ji, 0)
    const = lambda bi, ji: (0, 0)
    per_b = lambda bi, ji: (bi, 0, 0)
    return pl.pallas_call(
        functools.partial(_mem_attn_kernel, sub_rows=SUB_ROWS),
        grid=(b, s // ts),
        in_specs=[
            pl.BlockSpec((1, ts, d), tile),
            pl.BlockSpec((d, d), const, pipeline_mode=pl.Buffered(1)),
            pl.BlockSpec((1, nm, d), per_b),
            pl.BlockSpec((1, nm, d), per_b),
            pl.BlockSpec((d, d), const, pipeline_mode=pl.Buffered(1)),
            pl.BlockSpec((1, d), const),
            pl.BlockSpec((1, d), const),
        ],
        out_specs=pl.BlockSpec((1, ts, d), tile),
        out_shape=jax.ShapeDtypeStruct((b, s, d), F32),
        compiler_params=pltpu.CompilerParams(
            dimension_semantics=("arbitrary", "arbitrary"), vmem_limit_bytes=VMEM_LIMIT),
        name="mem_attn",
    )(h1, wq, km, vm, wo, g2, b2)


def _mlp_kernel(h_ref, w1_ref, b1_ref, w2_ref, b2_ref, g_ref, b_ref, o_ref, *, ff_chunk, sub_rows):
    for st in range(h_ref.shape[0] // sub_rows):
        rows = slice(st * sub_rows, (st + 1) * sub_rows)
        h = h_ref[rows, :]
        hb = h.astype(BF16)
        acc = ALPHA * h + b2_ref[...]
        for f in range(D_FF // ff_chunk):
            sl = slice(f * ff_chunk, (f + 1) * ff_chunk)
            a = jnp.dot(hb, w1_ref[:, sl], preferred_element_type=F32) + b1_ref[:, sl]
            a = jnp.maximum(a, 0.0)
            acc = acc + jnp.dot((a * a).astype(BF16), w2_ref[sl, :], preferred_element_type=F32)
        o_ref[rows, :] = _layernorm(acc, g_ref[...], b_ref[...])


def _mlp(h2, w1, b1, w2, b2, g3, b3, *, tm):
    m, d = h2.shape
    row = lambda i: (i, 0)
    const = lambda i: (0, 0)
    kern = functools.partial(_mlp_kernel, ff_chunk=1024, sub_rows=SUB_ROWS)
    return pl.pallas_call(
        kern,
        grid=(m // tm,),
        in_specs=[
            pl.BlockSpec((tm, d), row),
            pl.BlockSpec((d, D_FF), const, pipeline_mode=pl.Buffered(1)),
            pl.BlockSpec((1, D_FF), const),
            pl.BlockSpec((D_FF, d), const, pipeline_mode=pl.Buffered(1)),
            pl.BlockSpec((1, d), const),
            pl.BlockSpec((1, d), const),
            pl.BlockSpec((1, d), const),
        ],
        out_specs=pl.BlockSpec((tm, d), row),
        out_shape=jax.ShapeDtypeStruct((m, d), F32),
        compiler_params=pltpu.CompilerParams(
            dimension_semantics=("arbitrary",), vmem_limit_bytes=VMEM_LIMIT),
        name="mlp",
    )(h2, w1, b1, w2, b2, g3, b3)


def _rope_tables(seq_len, gain, scale):
    rows = seq_len // GRID_W
    row_ids = jnp.repeat(jnp.arange(rows, dtype=jnp.int32), GRID_W)
    col_ids = jnp.tile(jnp.arange(GRID_W, dtype=jnp.int32), rows)
    inv = ROPE_THETA ** (-jnp.arange(0, AXIS_DIM, 2, dtype=jnp.float32) / AXIS_DIM)
    ang = jnp.concatenate([row_ids[:, None].astype(jnp.float32) * inv,
                           col_ids[:, None].astype(jnp.float32) * inv], axis=-1)
    cos = jnp.repeat(jnp.cos(ang), 2, axis=-1)
    sin = jnp.repeat(jnp.sin(ang), 2, axis=-1)
    sign = jnp.where(jnp.arange(HEAD_DIM) % 2 == 0, -1.0, 1.0).astype(jnp.float32)
    gain_sw = gain.reshape(HEAD_DIM // 2, 2)[:, ::-1].reshape(HEAD_DIM)
    c = cos * gain * scale
    s = sin * sign * gain_sw * scale
    return jnp.tile(c, (1, LANES // HEAD_DIM)), jnp.tile(s, (1, LANES // HEAD_DIM))


def kernel(x, mem, ln_in_g, ln_in_b, w_in, q_norm_g, k_norm_g, conv_w, conv_b, conv_ln_g, conv_ln_b,
           attn_out_g, conv_out_g, w_out, ln1_g, ln1_b, w_mem_q, w_mem_kv, w_mem_o, ln2_g, ln2_b,
           w_ff1, b_ff1, w_ff2, b_ff2, ln3_g, ln3_b):
    b, s, d = x.shape
    assert d == D_MODEL and w_in.shape[0] == DEPTH == 1
    assert s % GRID_W == 0
    m = b * s
    nm = mem.shape[1]
    row = lambda a: a.reshape(1, -1)

    cq, sq = _rope_tables(s, q_norm_g[0], HEAD_DIM ** -0.5 * LOG2_E)
    ck, sk = _rope_tables(s, k_norm_g[0], 1.0)
    seg = jnp.arange(LANES) // HEAD_DIM
    bd = (seg[:, None] == seg[None, :]).astype(BF16)

    h, q, kd, vd, u = _in_proj(x.reshape(m, d), row(ln_in_g), row(ln_in_b), w_in[0].astype(BF16),
                               cq, sq, ck, sk, bd, seq=s, tm=IN_PROJ_ROWS)
    km, vm = _mem_kv(mem.reshape(b * nm, d), w_mem_kv[0].astype(BF16), tm=MEM_KV_ROWS)

    attn_n = _attention(q.reshape(b, s, ATTN_WIDTH), kd.reshape(b, s, 2 * LANES),
                        vd.reshape(b, s, 4 * LANES), row(attn_out_g[0]), tq=ATTN_Q_ROWS)

    h1 = _mix_out(u.reshape(b, s, CONV_WIDTH), attn_n, h.reshape(b, s, d), conv_w[0], row(conv_b[0]),
                  row(conv_ln_g[0]), row(conv_ln_b[0]), row(conv_out_g[0]), bd,
                  w_out[0].astype(BF16), row(ln1_g[0]), row(ln1_b[0]), ts=MIX_ROWS)

    h2 = _mem_attn(h1, w_mem_q[0].astype(BF16), km.reshape(b, nm, d), vm.reshape(b, nm, d),
                   w_mem_o[0].astype(BF16), row(ln2_g[0]), row(ln2_b[0]), ts=MEM_ATTN_ROWS)

    out = _mlp(h2.reshape(m, d), w_ff1[0].astype(BF16), row(b_ff1[0]), w_ff2[0].astype(BF16),
               row(b_ff2[0]), row(ln3_g[0]), row(ln3_b[0]), tm=MLP_ROWS)
    return out.reshape(b, s, d)
```

```python
import functools

import jax
import jax.numpy as jnp
from jax import lax
from jax.experimental import pallas as pl
from jax.experimental.pallas import tpu as pltpu

D_MODEL = 1024
HEAD_DIM = 64
ATTN_HEADS = 8
KV_HEADS = 2
ATTN_WIDTH = ATTN_HEADS * HEAD_DIM
KV_WIDTH = KV_HEADS * HEAD_DIM
CONV_WIDTH = D_MODEL - ATTN_WIDTH
CONV_K = 31
CONV_HALO = 16
MEM_HEADS = 4
MEM_HEAD_DIM = D_MODEL // MEM_HEADS
D_FF = 4 * D_MODEL
GRID_W = 64
AXIS_DIM = HEAD_DIM // 2
ROPE_THETA = 10000.0
EPS = 1e-5
DEPTH = 1
ALPHA = (2 * DEPTH) ** 0.25
LOG2_E = 1.4426950408889634

LANES = 128
F32_SUBLANES = 8
VMEM_LIMIT = 56 * 1024 * 1024

IN_PROJ_ROWS = 1024
MEM_KV_ROWS = 512
ATTN_Q_ROWS = 1024
MIX_ROWS = 1024
MEM_ATTN_ROWS = 1024
MLP_ROWS = 1024
SUB_ROWS = 256
CONV_CHUNK_ROWS = 32

F32 = jnp.float32
BF16 = jnp.bfloat16


def _layernorm(z, g, b):
    mu = jnp.mean(z, axis=-1, keepdims=True)
    zc = z - mu
    var = jnp.mean(zc * zc, axis=-1, keepdims=True)
    return zc * lax.rsqrt(var + EPS) * g + b


def _seg_mean_sq(z, bd):
    s = z * z
    hi = s.astype(BF16)
    lo = (s - hi.astype(F32)).astype(BF16)
    tot = (jnp.dot(hi, bd, preferred_element_type=F32)
           + jnp.dot(lo, bd, preferred_element_type=F32))
    return tot * (1.0 / HEAD_DIM)


def _in_proj_kernel(x_ref, g_ref, b_ref, w_ref, cq_ref, sq_ref, ck_ref, sk_ref, bd_ref,
                    h_ref, q_ref, kd_ref, vd_ref, u_ref, *, sub_rows):
    bd = bd_ref[...]
    lane = lax.broadcasted_iota(jnp.int32, (sub_rows, LANES), 1)
    even = (lane & 1) == 0
    lo = lane < HEAD_DIM
    ones = jnp.ones((sub_rows, LANES), BF16)
    c0 = ATTN_WIDTH + 2 * KV_WIDTH

    def norm_rope(z, c, s):
        sw = jnp.where(even, pltpu.roll(z, LANES - 1, 1), pltpu.roll(z, 1, 1))
        r = lax.rsqrt(_seg_mean_sq(z, bd) + EPS)
        return r * (z * c + sw * s)

    subs = [slice(st * sub_rows, (st + 1) * sub_rows) for st in range(x_ref.shape[0] // sub_rows)]
    hbs = []
    for rows in subs:
        h = _layernorm(x_ref[rows, :], g_ref[...], b_ref[...])
        h_ref[rows, :] = h
        hbs.append(h.astype(BF16))
    projs = []
    for hb in hbs:
        projs.append((
            jnp.dot(hb, w_ref[:, 0:ATTN_WIDTH], preferred_element_type=F32),
            jnp.dot(hb, w_ref[:, ATTN_WIDTH:c0], preferred_element_type=F32),
            jnp.dot(hb, w_ref[:, c0:c0 + CONV_WIDTH], preferred_element_type=F32),
            jnp.dot(hb, w_ref[:, c0 + CONV_WIDTH:c0 + 2 * CONV_WIDTH], preferred_element_type=F32)))
    for rows, (zq, zkv, val, gate) in zip(subs, projs):
        u_ref[rows, :] = val * jax.nn.sigmoid(gate)
        cq = cq_ref[rows, :]
        sq = sq_ref[rows, :]
        for g in range(ATTN_WIDTH // LANES):
            sl = slice(g * LANES, (g + 1) * LANES)
            q_ref[rows, sl] = norm_rope(zq[:, sl], cq, sq).astype(BF16)
        kk = norm_rope(zkv[:, 0:LANES], ck_ref[rows, :], sk_ref[rows, :])
        kr = pltpu.roll(kk, HEAD_DIM, 1)
        kd_ref[rows, 0:LANES] = jnp.where(lo, kk, kr).astype(BF16)
        kd_ref[rows, LANES:2 * LANES] = jnp.where(lo, kr, kk).astype(BF16)
        vv = zkv[:, LANES:2 * LANES]
        vr = pltpu.roll(vv, HEAD_DIM, 1)
        vd_ref[rows, 0:LANES] = jnp.where(lo, vv, vr).astype(BF16)
        vd_ref[rows, LANES:2 * LANES] = ones
        vd_ref[rows, 2 * LANES:3 * LANES] = jnp.where(lo, vr, vv).astype(BF16)
        vd_ref[rows, 3 * LANES:4 * LANES] = ones


def _in_proj(x2, ln_g, ln_b, w_in, cq, sq, ck, sk, bd, *, seq, tm):
    m, d = x2.shape
    n_in = w_in.shape[1]
    ns = seq // tm
    row = lambda i: (i, 0)
    const = lambda i: (0, 0)
    tab = lambda i: (i % ns, 0)
    return pl.pallas_call(
        functools.partial(_in_proj_kernel, sub_rows=SUB_ROWS),
        grid=(m // tm,),
        in_specs=[
            pl.BlockSpec((tm, d), row),
            pl.BlockSpec((1, d), const),
            pl.BlockSpec((1, d), const),
            pl.BlockSpec((d, n_in), const, pipeline_mode=pl.Buffered(1)),
            pl.BlockSpec((tm, LANES), tab),
            pl.BlockSpec((tm, LANES), tab),
            pl.BlockSpec((tm, LANES), tab),
            pl.BlockSpec((tm, LANES), tab),
            pl.BlockSpec((LANES, LANES), const),
        ],
        out_specs=[
            pl.BlockSpec((tm, d), row),
            pl.BlockSpec((tm, ATTN_WIDTH), row),
            pl.BlockSpec((tm, 2 * LANES), row),
            pl.BlockSpec((tm, 4 * LANES), row),
            pl.BlockSpec((tm, CONV_WIDTH), row),
        ],
        out_shape=[
            jax.ShapeDtypeStruct((m, d), F32),
            jax.ShapeDtypeStruct((m, ATTN_WIDTH), BF16),
            jax.ShapeDtypeStruct((m, 2 * LANES), BF16),
            jax.ShapeDtypeStruct((m, 4 * LANES), BF16),
            jax.ShapeDtypeStruct((m, CONV_WIDTH), F32),
        ],
        compiler_params=pltpu.CompilerParams(
            dimension_semantics=("arbitrary",), vmem_limit_bytes=VMEM_LIMIT),
        name="in_proj",
    )(x2, ln_g, ln_b, w_in, cq, sq, ck, sk, bd)


def _mem_kv_kernel(m_ref, w_ref, k_ref, v_ref):
    mb = m_ref[...].astype(BF16)
    k_ref[...] = jnp.dot(mb, w_ref[:, 0:D_MODEL], preferred_element_type=F32).astype(BF16)
    v_ref[...] = jnp.dot(mb, w_ref[:, D_MODEL:2 * D_MODEL], preferred_element_type=F32).astype(BF16)


def _mem_kv(mem2, w_kv, *, tm):
    m, d = mem2.shape
    row = lambda i: (i, 0)
    return pl.pallas_call(
        _mem_kv_kernel,
        grid=(m // tm,),
        in_specs=[pl.BlockSpec((tm, d), row), pl.BlockSpec((d, 2 * d), lambda i: (0, 0))],
        out_specs=[pl.BlockSpec((tm, d), row), pl.BlockSpec((tm, d), row)],
        out_shape=[jax.ShapeDtypeStruct((m, d), BF16), jax.ShapeDtypeStruct((m, d), BF16)],
        compiler_params=pltpu.CompilerParams(
            dimension_semantics=("arbitrary",), vmem_limit_bytes=VMEM_LIMIT),
        name="mem_kv",
    )(mem2, w_kv)


def _conv_chunk(upad_ref, wb_ref, bias, row0, rc):
    sub = F32_SUBLANES
    wrows = rc + 2 * CONV_HALO
    off0 = CONV_HALO - CONV_K // 2
    win = upad_ref[pl.ds(row0, wrows), :]
    acc = [bias] * (rc // sub)
    for r in range(sub):
        sh = win if r == 0 else pltpu.roll(win, wrows - r, 0)
        for a in range((2 * CONV_HALO) // sub):
            t = sub * a + r - off0
            if 0 <= t < CONV_K:
                w = wb_ref[t]
                for i in range(rc // sub):
                    lo_row = sub * (a + i)
                    acc[i] = acc[i] + sh[lo_row:lo_row + sub, :] * w
    return acc


def _attention_kernel(q_ref, k_ref, v_ref, g_ref, u_ref, cw_ref, cb_ref, o_ref, c_ref,
                      upad_ref, wb_ref, *, sub_rows, rows_per_chunk):
    tq = q_ref.shape[1]
    s = u_ref.shape[1]
    cw = u_ref.shape[2]
    j = pl.program_id(2)

    @pl.when(j == 0)
    def _():
        zeros = jnp.zeros((CONV_HALO, cw), F32)
        upad_ref[0:CONV_HALO, :] = zeros
        upad_ref[CONV_HALO + s:2 * CONV_HALO + s, :] = zeros
        upad_ref[CONV_HALO:CONV_HALO + s, :] = u_ref[0]
        for t in range(CONV_K):
            wb_ref[t] = jnp.broadcast_to(cw_ref[t:t + 1, :], (F32_SUBLANES, cw))

    q = q_ref[0]
    k = k_ref[0]
    v = v_ref[0]
    lane = lax.broadcasted_iota(jnp.int32, (sub_rows, LANES), 1)
    lo = lane < HEAD_DIM
    zero = jnp.zeros((sub_rows, LANES), q.dtype)
    bias = jnp.broadcast_to(cb_ref[...], (F32_SUBLANES, cw))
    rc = rows_per_chunk

    def one_head(qg, keep_lo):
        lhs = jnp.where(lo, qg, zero) if keep_lo else jnp.where(lo, zero, qg)
        sc = lax.dot_general(lhs, k, (((1,), (1,)), ((), ())), preferred_element_type=F32)
        m = jnp.max(sc, axis=-1, keepdims=True)
        p = jnp.exp2(sc - m).astype(BF16)
        r = jnp.dot(p, v, preferred_element_type=F32)
        o = r[:, 0:LANES] / r[:, LANES:2 * LANES]
        return o * lax.rsqrt(jnp.mean(o * o, axis=-1, keepdims=True) + EPS)

    for rb in range(tq // sub_rows):
        rows = slice(rb * sub_rows, (rb + 1) * sub_rows)
        for g in range(2):
            sl = slice(g * LANES, (g + 1) * LANES)
            qg = q[rows, sl]
            og = jnp.where(lo, one_head(qg, True), one_head(qg, False))
            o_ref[0, rows, sl] = (og * g_ref[:, sl]).astype(o_ref.dtype)
        for ci in range(sub_rows // rc):
            r0 = rb * sub_rows + ci * rc
            tiles = _conv_chunk(upad_ref, wb_ref, bias, pl.multiple_of(j * tq + r0, rc), rc)
            c_ref[0, r0:r0 + rc, :] = jnp.concatenate(tiles, axis=0)


def _attention(q, kd, vd, gain, u, conv_w, conv_b, *, tq):
    b, s, _ = q.shape
    cw = CONV_WIDTH // KV_HEADS
    tile = lambda bi, hi, ji: (bi, ji, hi)
    per_bh = lambda bi, hi, ji: (bi, 0, hi)
    per_h = lambda bi, hi, ji: (0, hi)
    kern = functools.partial(_attention_kernel, sub_rows=SUB_ROWS, rows_per_chunk=CONV_CHUNK_ROWS)
    return pl.pallas_call(
        kern,
        grid=(b, KV_HEADS, s // tq),
        in_specs=[
            pl.BlockSpec((1, tq, 2 * LANES), tile),
            pl.BlockSpec((1, s, LANES), per_bh),
            pl.BlockSpec((1, s, 2 * LANES), per_bh),
            pl.BlockSpec((1, 2 * LANES), per_h),
            pl.BlockSpec((1, s, cw), per_bh),
            pl.BlockSpec((CONV_K, cw), per_h),
            pl.BlockSpec((1, cw), per_h),
        ],
        out_specs=[
            pl.BlockSpec((1, tq, 2 * LANES), tile),
            pl.BlockSpec((1, tq, cw), tile),
        ],
        out_shape=[
            jax.ShapeDtypeStruct((b, s, ATTN_WIDTH), BF16),
            jax.ShapeDtypeStruct((b, s, CONV_WIDTH), F32),
        ],
        scratch_shapes=[
            pltpu.VMEM((s + 2 * CONV_HALO, cw), F32),
            pltpu.VMEM((CONV_K, F32_SUBLANES, cw), F32),
        ],
        compiler_params=pltpu.CompilerParams(
            dimension_semantics=("arbitrary", "arbitrary", "arbitrary"),
            vmem_limit_bytes=VMEM_LIMIT),
        name="attention",
    )(q, kd, vd, gain, u, conv_w, conv_b)


def _mix_out_kernel(c_ref, a_ref, h_ref, lg_ref, lb_ref, cog_ref, bd_ref,
                    wo_ref, g1_ref, b1_ref, o_ref, *, sub_rows):
    bd = bd_ref[...]
    subs = [slice(st * sub_rows, (st + 1) * sub_rows) for st in range(a_ref.shape[1] // sub_rows)]
    ycs = []
    for rows in subs:
        c = _layernorm(c_ref[0, rows, :], lg_ref[...], lb_ref[...])
        c = c * jax.nn.sigmoid(c)
        parts = []
        for g in range(CONV_WIDTH // LANES):
            sl = slice(g * LANES, (g + 1) * LANES)
            cg = c[:, sl]
            parts.append((cg * lax.rsqrt(_seg_mean_sq(cg, bd) + EPS) * cog_ref[:, sl]).astype(BF16))
        ycs.append(jnp.concatenate(parts, axis=1))
    mixes = []
    for rows, yc in zip(subs, ycs):
        mixes.append(jnp.dot(a_ref[0, rows, :], wo_ref[0:ATTN_WIDTH, :], preferred_element_type=F32)
                     + jnp.dot(yc, wo_ref[ATTN_WIDTH:D_MODEL, :], preferred_element_type=F32))
    for rows, mix in zip(subs, mixes):
        o_ref[0, rows, :] = _layernorm(ALPHA * h_ref[0, rows, :] + mix, g1_ref[...], b1_ref[...])


def _mix_out(conv, attn_n, h, ln_g, ln_b, out_g, bd, w_out, g1, b1, *, ts):
    b, s, d = h.shape
    tile = lambda bi, ji: (bi, ji, 0)
    const = lambda bi, ji: (0, 0)
    return pl.pallas_call(
        functools.partial(_mix_out_kernel, sub_rows=SUB_ROWS),
        grid=(b, s // ts),
        in_specs=[
            pl.BlockSpec((1, ts, CONV_WIDTH), tile),
            pl.BlockSpec((1, ts, ATTN_WIDTH), tile),
            pl.BlockSpec((1, ts, d), tile),
            pl.BlockSpec((1, CONV_WIDTH), const),
            pl.BlockSpec((1, CONV_WIDTH), const),
            pl.BlockSpec((1, CONV_WIDTH), const),
            pl.BlockSpec((LANES, LANES), const),
            pl.BlockSpec((d, d), const, pipeline_mode=pl.Buffered(1)),
            pl.BlockSpec((1, d), const),
            pl.BlockSpec((1, d), const),
        ],
        out_specs=pl.BlockSpec((1, ts, d), tile),
        out_shape=jax.ShapeDtypeStruct((b, s, d), F32),
        compiler_params=pltpu.CompilerParams(
            dimension_semantics=("arbitrary", "arbitrary"), vmem_limit_bytes=VMEM_LIMIT),
        name="mix_out",
    )(conv, attn_n, h, ln_g, ln_b, out_g, bd, w_out, g1, b1)


def _mem_attn_kernel(h_ref, wq_ref, k_ref, v_ref, wo_ref, g_ref, b_ref, o_ref, *, sub_rows):
    subs = [slice(st * sub_rows, (st + 1) * sub_rows) for st in range(h_ref.shape[1] // sub_rows)]
    heads = [slice(hd * MEM_HEAD_DIM, (hd + 1) * MEM_HEAD_DIM) for hd in range(MEM_HEADS)]
    scores = []
    for rows in subs:
        hb = h_ref[0, rows, :].astype(BF16)
        q = jnp.dot(hb, wq_ref[...], preferred_element_type=F32) * (MEM_HEAD_DIM ** -0.5)
        qb = q.astype(BF16)
        scores.append([lax.dot_general(qb[:, sl], k_ref[0, :, sl], (((1,), (1,)), ((), ())),
                                       preferred_element_type=F32) for sl in heads])
    atts = []
    for sc in scores:
        outs = []
        for s, sl in zip(sc, heads):
            m = jnp.max(s, axis=-1, keepdims=True)
            p = jnp.exp(s - m)
            l = jnp.sum(p, axis=-1, keepdims=True)
            o = jnp.dot(p.astype(BF16), v_ref[0, :, sl], preferred_element_type=F32) / l
            outs.append(o.astype(BF16))
        atts.append(jnp.dot(jnp.concatenate(outs, axis=1), wo_ref[...], preferred_element_type=F32))
    for rows, att in zip(subs, atts):
        o_ref[0, rows, :] = _layernorm(ALPHA * h_ref[0, rows, :] + att, g_ref[...], b_ref[...])


def _mem_attn(h1, wq, km, vm, wo, g2, b2, *, ts):
    b, s, d = h1.shape
    nm = km.shape[1]
    tile = lambda bi, ji: (bi, ji, 0)
    const = lambda bi, ji: (0, 0)
    per_b = lambda bi, ji: (bi, 0, 0)
    return pl.pallas_call(
        functools.partial(_mem_attn_kernel, sub_rows=SUB_ROWS),
        grid=(b, s // ts),
        in_specs=[
            pl.BlockSpec((1, ts, d), tile),
            pl.BlockSpec((d, d), const, pipeline_mode=pl.Buffered(1)),
            pl.BlockSpec((1, nm, d), per_b),
            pl.BlockSpec((1, nm, d), per_b),
            pl.BlockSpec((d, d), const, pipeline_mode=pl.Buffered(1)),
            pl.BlockSpec((1, d), const),
            pl.BlockSpec((1, d), const),
        ],
        out_specs=pl.BlockSpec((1, ts, d), tile),
        out_shape=jax.ShapeDtypeStruct((b, s, d), F32),
        compiler_params=pltpu.CompilerParams(
            dimension_semantics=("arbitrary", "arbitrary"), vmem_limit_bytes=VMEM_LIMIT),
        name="mem_attn",
    )(h1, wq, km, vm, wo, g2, b2)


def _mlp_kernel(h_ref, w1_ref, b1_ref, w2_ref, b2_ref, g_ref, b_ref, o_ref, *, ff_chunk, sub_rows):
    for st in range(h_ref.shape[0] // sub_rows):
        rows = slice(st * sub_rows, (st + 1) * sub_rows)
        h = h_ref[rows, :]
        hb = h.astype(BF16)
        acc = ALPHA * h + b2_ref[...]
        for f in range(D_FF // ff_chunk):
            sl = slice(f * ff_chunk, (f + 1) * ff_chunk)
            a = jnp.dot(hb, w1_ref[:, sl], preferred_element_type=F32) + b1_ref[:, sl]
            a = jnp.maximum(a, 0.0)
            acc = acc + jnp.dot((a * a).astype(BF16), w2_ref[sl, :], preferred_element_type=F32)
        o_ref[rows, :] = _layernorm(acc, g_ref[...], b_ref[...])


def _mlp(h2, w1, b1, w2, b2, g3, b3, *, tm):
    m, d = h2.shape
    row = lambda i: (i, 0)
    const = lambda i: (0, 0)
    kern = functools.partial(_mlp_kernel, ff_chunk=1024, sub_rows=SUB_ROWS)
    return pl.pallas_call(
        kern,
        grid=(m // tm,),
        in_specs=[
            pl.BlockSpec((tm, d), row),
            pl.BlockSpec((d, D_FF), const, pipeline_mode=pl.Buffered(1)),
            pl.BlockSpec((1, D_FF), const),
            pl.BlockSpec((D_FF, d), const, pipeline_mode=pl.Buffered(1)),
            pl.BlockSpec((1, d), const),
            pl.BlockSpec((1, d), const),
            pl.BlockSpec((1, d), const),
        ],
        out_specs=pl.BlockSpec((tm, d), row),
        out_shape=jax.ShapeDtypeStruct((m, d), F32),
        compiler_params=pltpu.CompilerParams(
            dimension_semantics=("arbitrary",), vmem_limit_bytes=VMEM_LIMIT),
        name="mlp",
    )(h2, w1, b1, w2, b2, g3, b3)


def _rope_tables(seq_len, gain, scale):
    rows = seq_len // GRID_W
    row_ids = jnp.repeat(jnp.arange(rows, dtype=jnp.int32), GRID_W)
    col_ids = jnp.tile(jnp.arange(GRID_W, dtype=jnp.int32), rows)
    inv = ROPE_THETA ** (-jnp.arange(0, AXIS_DIM, 2, dtype=jnp.float32) / AXIS_DIM)
    ang = jnp.concatenate([row_ids[:, None].astype(jnp.float32) * inv,
                           col_ids[:, None].astype(jnp.float32) * inv], axis=-1)
    cos = jnp.repeat(jnp.cos(ang), 2, axis=-1)
    sin = jnp.repeat(jnp.sin(ang), 2, axis=-1)
    sign = jnp.where(jnp.arange(HEAD_DIM) % 2 == 0, -1.0, 1.0).astype(jnp.float32)
    gain_sw = gain.reshape(HEAD_DIM // 2, 2)[:, ::-1].reshape(HEAD_DIM)
    c = cos * gain * scale
    s = sin * sign * gain_sw * scale
    return jnp.tile(c, (1, LANES // HEAD_DIM)), jnp.tile(s, (1, LANES // HEAD_DIM))


def kernel(x, mem, ln_in_g, ln_in_b, w_in, q_norm_g, k_norm_g, conv_w, conv_b, conv_ln_g, conv_ln_b,
           attn_out_g, conv_out_g, w_out, ln1_g, ln1_b, w_mem_q, w_mem_kv, w_mem_o, ln2_g, ln2_b,
           w_ff1, b_ff1, w_ff2, b_ff2, ln3_g, ln3_b):
    b, s, d = x.shape
    assert d == D_MODEL and w_in.shape[0] == DEPTH == 1
    assert s % GRID_W == 0
    m = b * s
    nm = mem.shape[1]
    row = lambda a: a.reshape(1, -1)

    cq, sq = _rope_tables(s, q_norm_g[0], HEAD_DIM ** -0.5 * LOG2_E)
    ck, sk = _rope_tables(s, k_norm_g[0], 1.0)
    seg = jnp.arange(LANES) // HEAD_DIM
    bd = (seg[:, None] == seg[None, :]).astype(BF16)

    h, q, kd, vd, u = _in_proj(x.reshape(m, d), row(ln_in_g), row(ln_in_b), w_in[0].astype(BF16),
                               cq, sq, ck, sk, bd, seq=s, tm=IN_PROJ_ROWS)
    km, vm = _mem_kv(mem.reshape(b * nm, d), w_mem_kv[0].astype(BF16), tm=MEM_KV_ROWS)

    attn_n, conv = _attention(q.reshape(b, s, ATTN_WIDTH), kd.reshape(b, s, 2 * LANES),
                              vd.reshape(b, s, 4 * LANES), row(attn_out_g[0]),
                              u.reshape(b, s, CONV_WIDTH), conv_w[0], row(conv_b[0]),
                              tq=ATTN_Q_ROWS)

    h1 = _mix_out(conv, attn_n, h.reshape(b, s, d), row(conv_ln_g[0]), row(conv_ln_b[0]),
                  row(conv_out_g[0]), bd, w_out[0].astype(BF16), row(ln1_g[0]), row(ln1_b[0]),
                  ts=MIX_ROWS)

    h2 = _mem_attn(h1, w_mem_q[0].astype(BF16), km.reshape(b, nm, d), vm.reshape(b, nm, d),
                   w_mem_o[0].astype(BF16), row(ln2_g[0]), row(ln2_b[0]), ts=MEM_ATTN_ROWS)

    out = _mlp(h2.reshape(m, d), w_ff1[0].astype(BF16), row(b_ff1[0]), w_ff2[0].astype(BF16),
               row(b_ff2[0]), row(ln3_g[0]), row(ln3_b[0]), tm=MLP_ROWS)
    return out.reshape(b, s, d)
```

```python
import functools

import jax
import jax.numpy as jnp
from jax import lax
from jax.experimental import pallas as pl
from jax.experimental.pallas import tpu as pltpu

D_MODEL = 1024
HEAD_DIM = 64
ATTN_HEADS = 8
KV_HEADS = 2
ATTN_WIDTH = ATTN_HEADS * HEAD_DIM
KV_WIDTH = KV_HEADS * HEAD_DIM
CONV_WIDTH = D_MODEL - ATTN_WIDTH
CONV_K = 31
CONV_HALO = 16
MEM_HEADS = 4
MEM_HEAD_DIM = D_MODEL // MEM_HEADS
D_FF = 4 * D_MODEL
GRID_W = 64
AXIS_DIM = HEAD_DIM // 2
ROPE_THETA = 10000.0
EPS = 1e-5
DEPTH = 1
ALPHA = (2 * DEPTH) ** 0.25
LOG2_E = 1.4426950408889634

LANES = 128
F32_SUBLANES = 8
VMEM_LIMIT = 56 * 1024 * 1024

IN_PROJ_ROWS = 1024
MEM_KV_ROWS = 1024
ATTN_Q_ROWS = 1024
MIX_ROWS = 1024
MEM_ATTN_ROWS = 1024
MLP_ROWS = 1024
SUB_ROWS = 256
CONV_CHUNK_ROWS = 32

F32 = jnp.float32
BF16 = jnp.bfloat16


def _layernorm(z, g, b):
    mu = jnp.mean(z, axis=-1, keepdims=True)
    zc = z - mu
    var = jnp.mean(zc * zc, axis=-1, keepdims=True)
    return zc * lax.rsqrt(var + EPS) * g + b


def _seg_mean_sq(z, bd):
    s = z * z
    hi = s.astype(BF16)
    lo = (s - hi.astype(F32)).astype(BF16)
    tot = (jnp.dot(hi, bd, preferred_element_type=F32)
           + jnp.dot(lo, bd, preferred_element_type=F32))
    return tot * (1.0 / HEAD_DIM)


def _in_proj_kernel(x_ref, g_ref, b_ref, w_ref, cq_ref, sq_ref, ck_ref, sk_ref, bd_ref,
                    h_ref, q_ref, kd_ref, vd_ref, u_ref, *, sub_rows):
    bd = bd_ref[...]
    lane = lax.broadcasted_iota(jnp.int32, (sub_rows, LANES), 1)
    even = (lane & 1) == 0
    lo = lane < HEAD_DIM
    ones = jnp.ones((sub_rows, LANES), BF16)
    c0 = ATTN_WIDTH + 2 * KV_WIDTH

    def norm_rope(z, c, s):
        sw = jnp.where(even, pltpu.roll(z, LANES - 1, 1), pltpu.roll(z, 1, 1))
        r = lax.rsqrt(_seg_mean_sq(z, bd) + EPS)
        return r * (z * c + sw * s)

    subs = [slice(st * sub_rows, (st + 1) * sub_rows) for st in range(x_ref.shape[0] // sub_rows)]
    hbs = []
    for rows in subs:
        h = _layernorm(x_ref[rows, :], g_ref[...], b_ref[...])
        h_ref[rows, :] = h
        hbs.append(h.astype(BF16))
    projs = []
    for hb in hbs:
        projs.append((
            jnp.dot(hb, w_ref[:, 0:ATTN_WIDTH], preferred_element_type=F32),
            jnp.dot(hb, w_ref[:, ATTN_WIDTH:c0], preferred_element_type=F32),
            jnp.dot(hb, w_ref[:, c0:c0 + CONV_WIDTH], preferred_element_type=F32),
            jnp.dot(hb, w_ref[:, c0 + CONV_WIDTH:c0 + 2 * CONV_WIDTH], preferred_element_type=F32)))
    for rows, (zq, zkv, val, gate) in zip(subs, projs):
        u_ref[rows, :] = val * jax.nn.sigmoid(gate)
        cq = cq_ref[rows, :]
        sq = sq_ref[rows, :]
        for g in range(ATTN_WIDTH // LANES):
            sl = slice(g * LANES, (g + 1) * LANES)
            q_ref[rows, sl] = norm_rope(zq[:, sl], cq, sq).astype(BF16)
        kk = norm_rope(zkv[:, 0:LANES], ck_ref[rows, :], sk_ref[rows, :])
        kr = pltpu.roll(kk, HEAD_DIM, 1)
        kd_ref[rows, 0:LANES] = jnp.where(lo, kk, kr).astype(BF16)
        kd_ref[rows, LANES:2 * LANES] = jnp.where(lo, kr, kk).astype(BF16)
        vv = zkv[:, LANES:2 * LANES]
        vr = pltpu.roll(vv, HEAD_DIM, 1)
        vd_ref[rows, 0:LANES] = jnp.where(lo, vv, vr).astype(BF16)
        vd_ref[rows, LANES:2 * LANES] = ones
        vd_ref[rows, 2 * LANES:3 * LANES] = jnp.where(lo, vr, vv).astype(BF16)
        vd_ref[rows, 3 * LANES:4 * LANES] = ones


def _in_proj(x2, ln_g, ln_b, w_in, cq, sq, ck, sk, bd, *, seq, tm):
    m, d = x2.shape
    n_in = w_in.shape[1]
    ns = seq // tm
    row = lambda i: (i, 0)
    const = lambda i: (0, 0)
    tab = lambda i: (i % ns, 0)
    return pl.pallas_call(
        functools.partial(_in_proj_kernel, sub_rows=SUB_ROWS),
        grid=(m // tm,),
        in_specs=[
            pl.BlockSpec((tm, d), row),
            pl.BlockSpec((1, d), const),
            pl.BlockSpec((1, d), const),
            pl.BlockSpec((d, n_in), const, pipeline_mode=pl.Buffered(1)),
            pl.BlockSpec((tm, LANES), tab),
            pl.BlockSpec((tm, LANES), tab),
            pl.BlockSpec((tm, LANES), tab),
            pl.BlockSpec((tm, LANES), tab),
            pl.BlockSpec((LANES, LANES), const),
        ],
        out_specs=[
            pl.BlockSpec((tm, d), row),
            pl.BlockSpec((tm, ATTN_WIDTH), row),
            pl.BlockSpec((tm, 2 * LANES), row),
            pl.BlockSpec((tm, 4 * LANES), row),
            pl.BlockSpec((tm, CONV_WIDTH), row),
        ],
        out_shape=[
            jax.ShapeDtypeStruct((m, d), F32),
            jax.ShapeDtypeStruct((m, ATTN_WIDTH), BF16),
            jax.ShapeDtypeStruct((m, 2 * LANES), BF16),
            jax.ShapeDtypeStruct((m, 4 * LANES), BF16),
            jax.ShapeDtypeStruct((m, CONV_WIDTH), F32),
        ],
        compiler_params=pltpu.CompilerParams(
            dimension_semantics=("arbitrary",), vmem_limit_bytes=VMEM_LIMIT),
        name="in_proj",
    )(x2, ln_g, ln_b, w_in, cq, sq, ck, sk, bd)


def _mem_kv_kernel(m_ref, w_ref, k_ref, v_ref):
    mb = m_ref[...].astype(BF16)
    wk = w_ref[:, 0:D_MODEL].astype(BF16)
    wv = w_ref[:, D_MODEL:2 * D_MODEL].astype(BF16)
    k_ref[...] = jnp.dot(mb, wk, preferred_element_type=F32).astype(BF16)
    v_ref[...] = jnp.dot(mb, wv, preferred_element_type=F32).astype(BF16)


def _mem_kv(mem2, w_kv, *, tm):
    m, d = mem2.shape
    row = lambda i: (i, 0)
    return pl.pallas_call(
        _mem_kv_kernel,
        grid=(m // tm,),
        in_specs=[pl.BlockSpec((tm, d), row),
                  pl.BlockSpec((d, 2 * d), lambda i: (0, 0), pipeline_mode=pl.Buffered(1))],
        out_specs=[pl.BlockSpec((tm, d), row), pl.BlockSpec((tm, d), row)],
        out_shape=[jax.ShapeDtypeStruct((m, d), BF16), jax.ShapeDtypeStruct((m, d), BF16)],
        compiler_params=pltpu.CompilerParams(
            dimension_semantics=("arbitrary",), vmem_limit_bytes=VMEM_LIMIT),
        name="mem_kv",
    )(mem2, w_kv)


def _conv_chunk(upad_ref, wb_ref, bias, row0, rc):
    sub = F32_SUBLANES
    wrows = rc + 2 * CONV_HALO
    off0 = CONV_HALO - CONV_K // 2
    win = upad_ref[pl.ds(row0, wrows), :]
    acc = [bias] * (rc // sub)
    for r in range(sub):
        sh = win if r == 0 else pltpu.roll(win, wrows - r, 0)
        for a in range((2 * CONV_HALO) // sub):
            t = sub * a + r - off0
            if 0 <= t < CONV_K:
                w = wb_ref[t]
                for i in range(rc // sub):
                    lo_row = sub * (a + i)
                    acc[i] = acc[i] + sh[lo_row:lo_row + sub, :] * w
    return acc


def _attention_kernel(q_ref, k_ref, v_ref, g_ref, u_ref, cw_ref, cb_ref, *rest,
                      sub_rows, rows_per_chunk, n_cast):
    w_refs = rest[:n_cast]
    o_ref, c_ref = rest[n_cast:n_cast + 2]
    wb16_refs = rest[n_cast + 2:2 * n_cast + 2]
    upad_ref, wb_ref = rest[2 * n_cast + 2:]
    for w_ref, wb16_ref in zip(w_refs, wb16_refs):
        wb16_ref[...] = w_ref[...].astype(wb16_ref.dtype)
    tq = q_ref.shape[1]
    s = u_ref.shape[1]
    cw = u_ref.shape[2]
    j = pl.program_id(2)

    @pl.when(j == 0)
    def _():
        zeros = jnp.zeros((CONV_HALO, cw), F32)
        upad_ref[0:CONV_HALO, :] = zeros
        upad_ref[CONV_HALO + s:2 * CONV_HALO + s, :] = zeros
        upad_ref[CONV_HALO:CONV_HALO + s, :] = u_ref[0]
        for t in range(CONV_K):
            wb_ref[t] = jnp.broadcast_to(cw_ref[t:t + 1, :], (F32_SUBLANES, cw))

    q = q_ref[0]
    k = k_ref[0]
    v = v_ref[0]
    lane = lax.broadcasted_iota(jnp.int32, (sub_rows, LANES), 1)
    lo = lane < HEAD_DIM
    zero = jnp.zeros((sub_rows, LANES), q.dtype)
    bias = jnp.broadcast_to(cb_ref[...], (F32_SUBLANES, cw))
    rc = rows_per_chunk

    def one_head(qg, keep_lo):
        lhs = jnp.where(lo, qg, zero) if keep_lo else jnp.where(lo, zero, qg)
        sc = lax.dot_general(lhs, k, (((1,), (1,)), ((), ())), preferred_element_type=F32)
        m = jnp.max(sc, axis=-1, keepdims=True)
        p = jnp.exp2(sc - m).astype(BF16)
        r = jnp.dot(p, v, preferred_element_type=F32)
        o = r[:, 0:LANES] / r[:, LANES:2 * LANES]
        return o * lax.rsqrt(jnp.mean(o * o, axis=-1, keepdims=True) + EPS)

    for rb in range(tq // sub_rows):
        rows = slice(rb * sub_rows, (rb + 1) * sub_rows)
        for g in range(2):
            sl = slice(g * LANES, (g + 1) * LANES)
            qg = q[rows, sl]
            og = jnp.where(lo, one_head(qg, True), one_head(qg, False))
            o_ref[0, rows, sl] = (og * g_ref[:, sl]).astype(o_ref.dtype)
        for ci in range(sub_rows // rc):
            r0 = rb * sub_rows + ci * rc
            tiles = _conv_chunk(upad_ref, wb_ref, bias, pl.multiple_of(j * tq + r0, rc), rc)
            c_ref[0, r0:r0 + rc, :] = jnp.concatenate(tiles, axis=0)


def _attention(q, kd, vd, gain, u, conv_w, conv_b, weights, *, tq):
    b, s, _ = q.shape
    nj = s // tq
    n_steps = b * KV_HEADS * nj
    cw = CONV_WIDTH // KV_HEADS
    tile = lambda bi, hi, ji: (bi, ji, hi)
    per_bh = lambda bi, hi, ji: (bi, 0, hi)
    per_h = lambda bi, hi, ji: (0, hi)
    slab = lambda bi, hi, ji: ((bi * KV_HEADS + hi) * nj + ji, 0)
    w_specs = [pl.BlockSpec((w.shape[0] // n_steps, w.shape[1]), slab) for w in weights]
    kern = functools.partial(_attention_kernel, sub_rows=SUB_ROWS, rows_per_chunk=CONV_CHUNK_ROWS,
                             n_cast=len(weights))
    outs = pl.pallas_call(
        kern,
        grid=(b, KV_HEADS, nj),
        in_specs=[
            pl.BlockSpec((1, tq, 2 * LANES), tile),
            pl.BlockSpec((1, s, LANES), per_bh),
            pl.BlockSpec((1, s, 2 * LANES), per_bh),
            pl.BlockSpec((1, 2 * LANES), per_h),
            pl.BlockSpec((1, s, cw), per_bh),
            pl.BlockSpec((CONV_K, cw), per_h),
            pl.BlockSpec((1, cw), per_h),
        ] + w_specs,
        out_specs=[
            pl.BlockSpec((1, tq, 2 * LANES), tile),
            pl.BlockSpec((1, tq, cw), tile),
        ] + w_specs,
        out_shape=[
            jax.ShapeDtypeStruct((b, s, ATTN_WIDTH), BF16),
            jax.ShapeDtypeStruct((b, s, CONV_WIDTH), F32),
        ] + [jax.ShapeDtypeStruct(w.shape, BF16) for w in weights],
        scratch_shapes=[
            pltpu.VMEM((s + 2 * CONV_HALO, cw), F32),
            pltpu.VMEM((CONV_K, F32_SUBLANES, cw), F32),
        ],
        compiler_params=pltpu.CompilerParams(
            dimension_semantics=("arbitrary", "arbitrary", "arbitrary"),
            vmem_limit_bytes=VMEM_LIMIT),
        name="attention",
    )(q, kd, vd, gain, u, conv_w, conv_b, *weights)
    return outs[0], outs[1], outs[2:]


def _mix_out_kernel(c_ref, a_ref, h_ref, lg_ref, lb_ref, cog_ref, bd_ref,
                    wo_ref, g1_ref, b1_ref, o_ref, *, sub_rows):
    bd = bd_ref[...]
    subs = [slice(st * sub_rows, (st + 1) * sub_rows) for st in range(a_ref.shape[1] // sub_rows)]
    ycs = []
    for rows in subs:
        c = _layernorm(c_ref[0, rows, :], lg_ref[...], lb_ref[...])
        c = c * jax.nn.sigmoid(c)
        parts = []
        for g in range(CONV_WIDTH // LANES):
            sl = slice(g * LANES, (g + 1) * LANES)
            cg = c[:, sl]
            parts.append((cg * lax.rsqrt(_seg_mean_sq(cg, bd) + EPS) * cog_ref[:, sl]).astype(BF16))
        ycs.append(jnp.concatenate(parts, axis=1))
    mixes = []
    for rows, yc in zip(subs, ycs):
        mixes.append(jnp.dot(a_ref[0, rows, :], wo_ref[0:ATTN_WIDTH, :], preferred_element_type=F32)
                     + jnp.dot(yc, wo_ref[ATTN_WIDTH:D_MODEL, :], preferred_element_type=F32))
    for rows, mix in zip(subs, mixes):
        o_ref[0, rows, :] = _layernorm(ALPHA * h_ref[0, rows, :] + mix, g1_ref[...], b1_ref[...])


def _mix_out(conv, attn_n, h, ln_g, ln_b, out_g, bd, w_out, g1, b1, *, ts):
    b, s, d = h.shape
    tile = lambda bi, ji: (bi, ji, 0)
    const = lambda bi, ji: (0, 0)
    return pl.pallas_call(
        functools.partial(_mix_out_kernel, sub_rows=SUB_ROWS),
        grid=(b, s // ts),
        in_specs=[
            pl.BlockSpec((1, ts, CONV_WIDTH), tile),
            pl.BlockSpec((1, ts, ATTN_WIDTH), tile),
            pl.BlockSpec((1, ts, d), tile),
            pl.BlockSpec((1, CONV_WIDTH), const),
            pl.BlockSpec((1, CONV_WIDTH), const),
            pl.BlockSpec((1, CONV_WIDTH), const),
            pl.BlockSpec((LANES, LANES), const),
            pl.BlockSpec((d, d), const, pipeline_mode=pl.Buffered(1)),
            pl.BlockSpec((1, d), const),
            pl.BlockSpec((1, d), const),
        ],
        out_specs=pl.BlockSpec((1, ts, d), tile),
        out_shape=jax.ShapeDtypeStruct((b, s, d), F32),
        compiler_params=pltpu.CompilerParams(
            dimension_semantics=("arbitrary", "arbitrary"), vmem_limit_bytes=VMEM_LIMIT),
        name="mix_out",
    )(conv, attn_n, h, ln_g, ln_b, out_g, bd, w_out, g1, b1)


def _mem_attn_kernel(h_ref, wq_ref, k_ref, v_ref, wo_ref, g_ref, b_ref, o_ref, *, sub_rows):
    subs = [slice(st * sub_rows, (st + 1) * sub_rows) for st in range(h_ref.shape[1] // sub_rows)]
    heads = [slice(hd * MEM_HEAD_DIM, (hd + 1) * MEM_HEAD_DIM) for hd in range(MEM_HEADS)]
    scores = []
    for rows in subs:
        hb = h_ref[0, rows, :].astype(BF16)
        q = jnp.dot(hb, wq_ref[...], preferred_element_type=F32) * (MEM_HEAD_DIM ** -0.5)
        qb = q.astype(BF16)
        scores.append([lax.dot_general(qb[:, sl], k_ref[0, :, sl], (((1,), (1,)), ((), ())),
                                       preferred_element_type=F32) for sl in heads])
    atts = []
    for sc in scores:
        outs = []
        for s, sl in zip(sc, heads):
            m = jnp.max(s, axis=-1, keepdims=True)
            p = jnp.exp(s - m)
            l = jnp.sum(p, axis=-1, keepdims=True)
            o = jnp.dot(p.astype(BF16), v_ref[0, :, sl], preferred_element_type=F32) / l
            outs.append(o.astype(BF16))
        atts.append(jnp.dot(jnp.concatenate(outs, axis=1), wo_ref[...], preferred_element_type=F32))
    for rows, att in zip(subs, atts):
        o_ref[0, rows, :] = _layernorm(ALPHA * h_ref[0, rows, :] + att, g_ref[...], b_ref[...])


def _mem_attn(h1, wq, km, vm, wo, g2, b2, *, ts):
    b, s, d = h1.shape
    nm = km.shape[1]
    tile = lambda bi, ji: (bi, ji, 0)
    const = lambda bi, ji: (0, 0)
    per_b = lambda bi, ji: (bi, 0, 0)
    return pl.pallas_call(
        functools.partial(_mem_attn_kernel, sub_rows=SUB_ROWS),
        grid=(b, s // ts),
        in_specs=[
            pl.BlockSpec((1, ts, d), tile),
            pl.BlockSpec((d, d), const, pipeline_mode=pl.Buffered(1)),
            pl.BlockSpec((1, nm, d), per_b),
            pl.BlockSpec((1, nm, d), per_b),
            pl.BlockSpec((d, d), const, pipeline_mode=pl.Buffered(1)),
            pl.BlockSpec((1, d), const),
            pl.BlockSpec((1, d), const),
        ],
        out_specs=pl.BlockSpec((1, ts, d), tile),
        out_shape=jax.ShapeDtypeStruct((b, s, d), F32),
        compiler_params=pltpu.CompilerParams(
            dimension_semantics=("arbitrary", "arbitrary"), vmem_limit_bytes=VMEM_LIMIT),
        name="mem_attn",
    )(h1, wq, km, vm, wo, g2, b2)


def _mlp_kernel(h_ref, w1_ref, b1_ref, w2_ref, b2_ref, g_ref, b_ref, o_ref, *, ff_chunk, sub_rows):
    for st in range(h_ref.shape[0] // sub_rows):
        rows = slice(st * sub_rows, (st + 1) * sub_rows)
        h = h_ref[rows, :]
        hb = h.astype(BF16)
        acc = ALPHA * h + b2_ref[...]
        for f in range(D_FF // ff_chunk):
            sl = slice(f * ff_chunk, (f + 1) * ff_chunk)
            a = jnp.dot(hb, w1_ref[:, sl], preferred_element_type=F32) + b1_ref[:, sl]
            a = jnp.maximum(a, 0.0)
            acc = acc + jnp.dot((a * a).astype(BF16), w2_ref[sl, :], preferred_element_type=F32)
        o_ref[rows, :] = _layernorm(acc, g_ref[...], b_ref[...])


def _mlp(h2, w1, b1, w2, b2, g3, b3, *, tm):
    m, d = h2.shape
    row = lambda i: (i, 0)
    const = lambda i: (0, 0)
    kern = functools.partial(_mlp_kernel, ff_chunk=1024, sub_rows=SUB_ROWS)
    return pl.pallas_call(
        kern,
        grid=(m // tm,),
        in_specs=[
            pl.BlockSpec((tm, d), row),
            pl.BlockSpec((d, D_FF), const, pipeline_mode=pl.Buffered(1)),
            pl.BlockSpec((1, D_FF), const),
            pl.BlockSpec((D_FF, d), const, pipeline_mode=pl.Buffered(1)),
            pl.BlockSpec((1, d), const),
            pl.BlockSpec((1, d), const),
            pl.BlockSpec((1, d), const),
        ],
        out_specs=pl.BlockSpec((tm, d), row),
        out_shape=jax.ShapeDtypeStruct((m, d), F32),
        compiler_params=pltpu.CompilerParams(
            dimension_semantics=("arbitrary",), vmem_limit_bytes=VMEM_LIMIT),
        name="mlp",
    )(h2, w1, b1, w2, b2, g3, b3)


def _rope_tables(seq_len, gain, scale):
    rows = seq_len // GRID_W
    row_ids = jnp.repeat(jnp.arange(rows, dtype=jnp.int32), GRID_W)
    col_ids = jnp.tile(jnp.arange(GRID_W, dtype=jnp.int32), rows)
    inv = ROPE_THETA ** (-jnp.arange(0, AXIS_DIM, 2, dtype=jnp.float32) / AXIS_DIM)
    ang = jnp.concatenate([row_ids[:, None].astype(jnp.float32) * inv,
                           col_ids[:, None].astype(jnp.float32) * inv], axis=-1)
    cos = jnp.repeat(jnp.cos(ang), 2, axis=-1)
    sin = jnp.repeat(jnp.sin(ang), 2, axis=-1)
    sign = jnp.where(jnp.arange(HEAD_DIM) % 2 == 0, -1.0, 1.0).astype(jnp.float32)
    gain_sw = gain.reshape(HEAD_DIM // 2, 2)[:, ::-1].reshape(HEAD_DIM)
    c = cos * gain * scale
    s = sin * sign * gain_sw * scale
    return jnp.tile(c, (1, LANES // HEAD_DIM)), jnp.tile(s, (1, LANES // HEAD_DIM))


def kernel(x, mem, ln_in_g, ln_in_b, w_in, q_norm_g, k_norm_g, conv_w, conv_b, conv_ln_g, conv_ln_b,
           attn_out_g, conv_out_g, w_out, ln1_g, ln1_b, w_mem_q, w_mem_kv, w_mem_o, ln2_g, ln2_b,
           w_ff1, b_ff1, w_ff2, b_ff2, ln3_g, ln3_b):
    b, s, d = x.shape
    assert d == D_MODEL and w_in.shape[0] == DEPTH == 1
    assert s % GRID_W == 0
    m = b * s
    nm = mem.shape[1]
    row = lambda a: a.reshape(1, -1)

    cq, sq = _rope_tables(s, q_norm_g[0], HEAD_DIM ** -0.5 * LOG2_E)
    ck, sk = _rope_tables(s, k_norm_g[0], 1.0)
    seg = jnp.arange(LANES) // HEAD_DIM
    bd = (seg[:, None] == seg[None, :]).astype(BF16)

    h, q, kd, vd, u = _in_proj(x.reshape(m, d), row(ln_in_g), row(ln_in_b), w_in[0].astype(BF16),
                               cq, sq, ck, sk, bd, seq=s, tm=IN_PROJ_ROWS)
    km, vm = _mem_kv(mem.reshape(b * nm, d), w_mem_kv[0], tm=MEM_KV_ROWS)

    attn_n, conv, (w_out_b, w_mq_b, w_mo_b, w_ff1_b, w_ff2_b) = _attention(
        q.reshape(b, s, ATTN_WIDTH), kd.reshape(b, s, 2 * LANES), vd.reshape(b, s, 4 * LANES),
        row(attn_out_g[0]), u.reshape(b, s, CONV_WIDTH), conv_w[0], row(conv_b[0]),
        [w_out[0], w_mem_q[0], w_mem_o[0], w_ff1[0], w_ff2[0]], tq=ATTN_Q_ROWS)

    h1 = _mix_out(conv, attn_n, h.reshape(b, s, d), row(conv_ln_g[0]), row(conv_ln_b[0]),
                  row(conv_out_g[0]), bd, w_out_b, row(ln1_g[0]), row(ln1_b[0]), ts=MIX_ROWS)

    h2 = _mem_attn(h1, w_mq_b, km.reshape(b, nm, d), vm.reshape(b, nm, d), w_mo_b,
                   row(ln2_g[0]), row(ln2_b[0]), ts=MEM_ATTN_ROWS)

    out = _mlp(h2.reshape(m, d), w_ff1_b, row(b_ff1[0]), w_ff2_b, row(b_ff2[0]),
               row(ln3_g[0]), row(ln3_b[0]), tm=MLP_ROWS)
    return out.reshape(b, s, d)
```

```python
import functools

import jax
import jax.numpy as jnp
from jax import lax
from jax.experimental import pallas as pl
from jax.experimental.pallas import tpu as pltpu

D_MODEL = 1024
HEAD_DIM = 64
ATTN_HEADS = 8
KV_HEADS = 2
ATTN_WIDTH = ATTN_HEADS * HEAD_DIM
KV_WIDTH = KV_HEADS * HEAD_DIM
CONV_WIDTH = D_MODEL - ATTN_WIDTH
CONV_K = 31
CONV_HALO = 16
MEM_HEADS = 4
MEM_HEAD_DIM = D_MODEL // MEM_HEADS
D_FF = 4 * D_MODEL
GRID_W = 64
AXIS_DIM = HEAD_DIM // 2
ROPE_THETA = 10000.0
EPS = 1e-5
DEPTH = 1
ALPHA = (2 * DEPTH) ** 0.25
LOG2_E = 1.4426950408889634

LANES = 128
F32_SUBLANES = 8
VMEM_LIMIT = 56 * 1024 * 1024

IN_PROJ_ROWS = 1024
MEM_KV_ROWS = 1024
ATTN_Q_ROWS = 2048
MIX_ROWS = 1024
MEM_ATTN_ROWS = 1024
MLP_ROWS = 1024
SUB_ROWS = 256
CONV_CHUNK_ROWS = 32
CONV_LAG_CHAINS = 4
SAFE_EXP2_BOUND = 60.0

F32 = jnp.float32
BF16 = jnp.bfloat16


def _layernorm(z, g, b):
    mu = jnp.mean(z, axis=-1, keepdims=True)
    zc = z - mu
    var = jnp.mean(zc * zc, axis=-1, keepdims=True)
    return zc * lax.rsqrt(var + EPS) * g + b


def _seg_mean_sq(z, bd):
    s = z * z
    hi = s.astype(BF16)
    lo = (s - hi.astype(F32)).astype(BF16)
    tot = (jnp.dot(hi, bd, preferred_element_type=F32)
           + jnp.dot(lo, bd, preferred_element_type=F32))
    return tot * (1.0 / HEAD_DIM)


def _in_proj_kernel(x_ref, g_ref, b_ref, w_ref, cq_ref, sq_ref, ck_ref, sk_ref, bd_ref,
                    h_ref, q_ref, kd_ref, vd_ref, u_ref, *, sub_rows):
    bd = bd_ref[...]
    lane = lax.broadcasted_iota(jnp.int32, (sub_rows, LANES), 1)
    even = (lane & 1) == 0
    lo = lane < HEAD_DIM
    ones = jnp.ones((sub_rows, LANES), BF16)
    c0 = ATTN_WIDTH + 2 * KV_WIDTH

    def norm_rope(z, c, s):
        sw = jnp.where(even, pltpu.roll(z, LANES - 1, 1), pltpu.roll(z, 1, 1))
        r = lax.rsqrt(_seg_mean_sq(z, bd) + EPS)
        return r * (z * c + sw * s)

    subs = [slice(st * sub_rows, (st + 1) * sub_rows) for st in range(x_ref.shape[0] // sub_rows)]
    hbs = []
    for rows in subs:
        h = _layernorm(x_ref[rows, :], g_ref[...], b_ref[...])
        h_ref[rows, :] = h
        hbs.append(h.astype(BF16))
    projs = []
    for hb in hbs:
        projs.append((
            jnp.dot(hb, w_ref[:, 0:ATTN_WIDTH], preferred_element_type=F32),
            jnp.dot(hb, w_ref[:, ATTN_WIDTH:c0], preferred_element_type=F32),
            jnp.dot(hb, w_ref[:, c0:c0 + CONV_WIDTH], preferred_element_type=F32),
            jnp.dot(hb, w_ref[:, c0 + CONV_WIDTH:c0 + 2 * CONV_WIDTH], preferred_element_type=F32)))
    for rows, (zq, zkv, val, gate) in zip(subs, projs):
        u_ref[rows, :] = val * jax.nn.sigmoid(gate)
        cq = cq_ref[rows, :]
        sq = sq_ref[rows, :]
        for g in range(ATTN_WIDTH // LANES):
            sl = slice(g * LANES, (g + 1) * LANES)
            q_ref[rows, sl] = norm_rope(zq[:, sl], cq, sq).astype(BF16)
        kk = norm_rope(zkv[:, 0:LANES], ck_ref[rows, :], sk_ref[rows, :])
        kr = pltpu.roll(kk, HEAD_DIM, 1)
        kd_ref[rows, 0:LANES] = jnp.where(lo, kk, kr).astype(BF16)
        kd_ref[rows, LANES:2 * LANES] = jnp.where(lo, kr, kk).astype(BF16)
        vv = zkv[:, LANES:2 * LANES]
        vr = pltpu.roll(vv, HEAD_DIM, 1)
        vd_ref[rows, 0:LANES] = jnp.where(lo, vv, vr).astype(BF16)
        vd_ref[rows, LANES:2 * LANES] = ones
        vd_ref[rows, 2 * LANES:3 * LANES] = jnp.where(lo, vr, vv).astype(BF16)
        vd_ref[rows, 3 * LANES:4 * LANES] = ones


def _in_proj(x2, ln_g, ln_b, w_in, cq, sq, ck, sk, bd, *, seq, tm):
    m, d = x2.shape
    n_in = w_in.shape[1]
    ns = seq // tm
    row = lambda i: (i, 0)
    const = lambda i: (0, 0)
    tab = lambda i: (i % ns, 0)
    return pl.pallas_call(
        functools.partial(_in_proj_kernel, sub_rows=SUB_ROWS),
        grid=(m // tm,),
        in_specs=[
            pl.BlockSpec((tm, d), row),
            pl.BlockSpec((1, d), const),
            pl.BlockSpec((1, d), const),
            pl.BlockSpec((d, n_in), const, pipeline_mode=pl.Buffered(1)),
            pl.BlockSpec((tm, LANES), tab),
            pl.BlockSpec((tm, LANES), tab),
            pl.BlockSpec((tm, LANES), tab),
            pl.BlockSpec((tm, LANES), tab),
            pl.BlockSpec((LANES, LANES), const),
        ],
        out_specs=[
            pl.BlockSpec((tm, d), row),
            pl.BlockSpec((tm, ATTN_WIDTH), row),
            pl.BlockSpec((tm, 2 * LANES), row),
            pl.BlockSpec((tm, 4 * LANES), row),
            pl.BlockSpec((tm, CONV_WIDTH), row),
        ],
        out_shape=[
            jax.ShapeDtypeStruct((m, d), F32),
            jax.ShapeDtypeStruct((m, ATTN_WIDTH), BF16),
            jax.ShapeDtypeStruct((m, 2 * LANES), BF16),
            jax.ShapeDtypeStruct((m, 4 * LANES), BF16),
            jax.ShapeDtypeStruct((m, CONV_WIDTH), F32),
        ],
        compiler_params=pltpu.CompilerParams(
            dimension_semantics=("arbitrary",), vmem_limit_bytes=VMEM_LIMIT),
        name="in_proj",
    )(x2, ln_g, ln_b, w_in, cq, sq, ck, sk, bd)


def _mem_kv_kernel(m_ref, w_ref, k_ref, v_ref):
    mb = m_ref[...].astype(BF16)
    wk = w_ref[:, 0:D_MODEL].astype(BF16)
    wv = w_ref[:, D_MODEL:2 * D_MODEL].astype(BF16)
    k_ref[...] = jnp.dot(mb, wk, preferred_element_type=F32).astype(BF16)
    v_ref[...] = jnp.dot(mb, wv, preferred_element_type=F32).astype(BF16)


def _mem_kv(mem2, w_kv, *, tm):
    m, d = mem2.shape
    row = lambda i: (i, 0)
    return pl.pallas_call(
        _mem_kv_kernel,
        grid=(m // tm,),
        in_specs=[pl.BlockSpec((tm, d), row),
                  pl.BlockSpec((d, 2 * d), lambda i: (0, 0), pipeline_mode=pl.Buffered(1))],
        out_specs=[pl.BlockSpec((tm, d), row), pl.BlockSpec((tm, d), row)],
        out_shape=[jax.ShapeDtypeStruct((m, d), BF16), jax.ShapeDtypeStruct((m, d), BF16)],
        compiler_params=pltpu.CompilerParams(
            dimension_semantics=("arbitrary",), vmem_limit_bytes=VMEM_LIMIT),
        name="mem_kv",
    )(mem2, w_kv)


def _conv_chunk(upad_ref, wb_ref, bias, row0, rc):
    sub = F32_SUBLANES
    wrows = rc + 2 * CONV_HALO
    off0 = CONV_HALO - CONV_K // 2
    win = upad_ref[pl.ds(row0, wrows), :]
    acc = [bias] * (rc // sub)
    for r in range(sub):
        sh = win if r == 0 else pltpu.roll(win, wrows - r, 0)
        for a in range((2 * CONV_HALO) // sub):
            t = sub * a + r - off0
            if 0 <= t < CONV_K:
                w = wb_ref[t]
                for i in range(rc // sub):
                    lo_row = sub * (a + i)
                    acc[i] = acc[i] + sh[lo_row:lo_row + sub, :] * w
    return acc


def _attention_kernel(fast_ref, q_ref, k_ref, v_ref, g_ref, u_ref, cw_ref, cb_ref, *rest,
                      sub_rows, rows_per_chunk, n_cast, conv_lag):
    w_refs = rest[:n_cast]
    o_ref, c_ref = rest[n_cast:n_cast + 2]
    wb16_refs = rest[n_cast + 2:2 * n_cast + 2]
    upad_ref, wb_ref = rest[2 * n_cast + 2:]
    for w_ref, wb16_ref in zip(w_refs, wb16_refs):
        wb16_ref[...] = w_ref[...].astype(wb16_ref.dtype)
    tq = q_ref.shape[1]
    s = u_ref.shape[1]
    cw = u_ref.shape[2]
    j = pl.program_id(2)

    @pl.when(j == 0)
    def _():
        zeros = jnp.zeros((CONV_HALO, cw), F32)
        upad_ref[0:CONV_HALO, :] = zeros
        upad_ref[CONV_HALO + s:2 * CONV_HALO + s, :] = zeros
        upad_ref[CONV_HALO:CONV_HALO + s, :] = u_ref[0]
        for t in range(CONV_K):
            wb_ref[t] = jnp.broadcast_to(cw_ref[t:t + 1, :], (F32_SUBLANES, cw))

    q = q_ref[0]
    k = k_ref[0]
    v = v_ref[0]
    lane = lax.broadcasted_iota(jnp.int32, (sub_rows, LANES), 1)
    lo = lane < HEAD_DIM
    never = lane >= LANES + j
    bias = jnp.broadcast_to(cb_ref[...], (F32_SUBLANES, cw))
    rc = rows_per_chunk
    chains = [(rb, g, keep_lo) for rb in range(tq // sub_rows) for g in range(2)
              for keep_lo in (True, False)]
    n_chunks = tq // rc

    def conv_chunk(ci):
        r0 = ci * rc
        tiles = _conv_chunk(upad_ref, wb_ref, bias, pl.multiple_of(j * tq + r0, rc), rc)
        c_ref[0, r0:r0 + rc, :] = jnp.concatenate(tiles, axis=0)
        return tiles

    def zero_after(tiles):
        z = jnp.zeros((sub_rows, LANES), F32)
        if tiles:
            t = functools.reduce(lambda x, y: x + y, tiles)
            t = functools.reduce(lambda x, y: x + y,
                                 [t[:, c * LANES:(c + 1) * LANES] for c in range(cw // LANES)])
            z = jnp.where(never, jnp.broadcast_to(t[0:1, :], z.shape), z)
        return z.astype(q.dtype)

    def one_head(qg, keep_lo, zero, shift_by_max):
        lhs = jnp.where(lo, qg, zero) if keep_lo else jnp.where(lo, zero, qg)
        sc = lax.dot_general(lhs, k, (((1,), (1,)), ((), ())), preferred_element_type=F32)
        if shift_by_max:
            sc = sc - jnp.max(sc, axis=-1, keepdims=True)
        p = jnp.exp2(sc).astype(BF16)
        r = jnp.dot(p, v, preferred_element_type=F32)
        o = r[:, 0:LANES] / r[:, LANES:2 * LANES]
        return o * lax.rsqrt(jnp.mean(o * o, axis=-1, keepdims=True) + EPS)

    def attend(shift_by_max, groups):
        fed = []
        done = {}
        for ci, (rb, g, keep_lo) in enumerate(chains):
            fed.append([t for c in groups[ci] for t in conv_chunk(c)])
            rows = slice(rb * sub_rows, (rb + 1) * sub_rows)
            sl = slice(g * LANES, (g + 1) * LANES)
            zero = zero_after(fed[ci - conv_lag] if ci >= conv_lag else [])
            done[keep_lo] = one_head(q[rows, sl], keep_lo, zero, shift_by_max)
            if not keep_lo:
                og = jnp.where(lo, done[True], done[False])
                o_ref[0, rows, sl] = (og * g_ref[:, sl]).astype(o_ref.dtype)

    fast = fast_ref[0] > 0
    n_fed = len(chains) - conv_lag
    spread = [list(range((n_chunks * i) // n_fed, (n_chunks * (i + 1)) // n_fed)) if i < n_fed
              else [] for i in range(len(chains))]
    per_block = n_chunks // (tq // sub_rows)
    blocked = [list(range(i // 4 * per_block, (i // 4 + 1) * per_block)) if i % 4 == 3 else []
               for i in range(len(chains))]

    @pl.when(fast)
    def _():
        attend(False, spread)

    @pl.when(jnp.logical_not(fast))
    def _():
        attend(True, blocked)


def _attention(fast, q, kd, vd, gain, u, conv_w, conv_b, weights, *, tq):
    b, s, _ = q.shape
    nj = s // tq
    n_steps = b * KV_HEADS * nj
    cw = CONV_WIDTH // KV_HEADS
    tile = lambda bi, hi, ji: (bi, ji, hi)
    per_bh = lambda bi, hi, ji: (bi, 0, hi)
    per_h = lambda bi, hi, ji: (0, hi)
    slab = lambda bi, hi, ji: ((bi * KV_HEADS + hi) * nj + ji, 0)
    w_specs = [pl.BlockSpec((w.shape[0] // n_steps, w.shape[1]), slab) for w in weights]
    kern = functools.partial(_attention_kernel, sub_rows=SUB_ROWS, rows_per_chunk=CONV_CHUNK_ROWS,
                             n_cast=len(weights), conv_lag=CONV_LAG_CHAINS)
    outs = pl.pallas_call(
        kern,
        grid=(b, KV_HEADS, nj),
        in_specs=[
            pl.BlockSpec(memory_space=pltpu.SMEM),
            pl.BlockSpec((1, tq, 2 * LANES), tile),
            pl.BlockSpec((1, s, LANES), per_bh),
            pl.BlockSpec((1, s, 2 * LANES), per_bh),
            pl.BlockSpec((1, 2 * LANES), per_h),
            pl.BlockSpec((1, s, cw), per_bh),
            pl.BlockSpec((CONV_K, cw), per_h),
            pl.BlockSpec((1, cw), per_h),
        ] + w_specs,
        out_specs=[
            pl.BlockSpec((1, tq, 2 * LANES), tile),
            pl.BlockSpec((1, tq, cw), tile),
        ] + w_specs,
        out_shape=[
            jax.ShapeDtypeStruct((b, s, ATTN_WIDTH), BF16),
            jax.ShapeDtypeStruct((b, s, CONV_WIDTH), F32),
        ] + [jax.ShapeDtypeStruct(w.shape, BF16) for w in weights],
        scratch_shapes=[
            pltpu.VMEM((s + 2 * CONV_HALO, cw), F32),
            pltpu.VMEM((CONV_K, F32_SUBLANES, cw), F32),
        ],
        compiler_params=pltpu.CompilerParams(
            dimension_semantics=("arbitrary", "arbitrary", "arbitrary"),
            vmem_limit_bytes=VMEM_LIMIT),
        name="attention",
    )(fast, q, kd, vd, gain, u, conv_w, conv_b, *weights)
    return outs[0], outs[1], outs[2:]


def _seg_mean_sq_lanes(z):
    lo = lax.broadcasted_iota(jnp.int32, z.shape, 1) < HEAD_DIM
    s = z * z
    s_lo = jnp.sum(jnp.where(lo, s, 0.0), axis=-1, keepdims=True)
    s_hi = jnp.sum(jnp.where(lo, 0.0, s), axis=-1, keepdims=True)
    return jnp.where(lo, s_lo, s_hi) * (1.0 / HEAD_DIM)


def _mix_out_kernel(c_ref, a_ref, h_ref, lg_ref, lb_ref, cog_ref,
                    wo_ref, g1_ref, b1_ref, o_ref, *, sub_rows):
    subs =[slice(st * sub_rows, (st + 1) * sub_rows) for st in range(a_ref.shape[1] // sub_rows)]
    ycs = []
    for rows in subs:
        c = _layernorm(c_ref[0, rows, :], lg_ref[...], lb_ref[...])
        c = c * jax.nn.sigmoid(c)
        parts = []
        for g in range(CONV_WIDTH // LANES):
            sl = slice(g * LANES, (g + 1) * LANES)
            cg = c[:, sl]
            parts.append((cg * lax.rsqrt(_seg_mean_sq_lanes(cg) + EPS) * cog_ref[:, sl]).astype(BF16))
        ycs.append(jnp.concatenate(parts, axis=1))
    mixes = []
    for rows, yc in zip(subs, ycs):
        mixes.append(jnp.dot(a_ref[0, rows, :], wo_ref[0:ATTN_WIDTH, :], preferred_element_type=F32)
                     + jnp.dot(yc, wo_ref[ATTN_WIDTH:D_MODEL, :], preferred_element_type=F32))
    for rows, mix in zip(subs, mixes):
        o_ref[0, rows, :] = _layernorm(ALPHA * h_ref[0, rows, :] + mix, g1_ref[...], b1_ref[...])


def _mix_out(conv, attn_n, h, ln_g, ln_b, out_g, w_out, g1, b1, *, ts):
    b, s, d = h.shape
    tile = lambda bi, ji: (bi, ji, 0)
    const = lambda bi, ji: (0, 0)
    return pl.pallas_call(
        functools.partial(_mix_out_kernel, sub_rows=SUB_ROWS),
        grid=(b, s // ts),
        in_specs=[
            pl.BlockSpec((1, ts, CONV_WIDTH), tile),
            pl.BlockSpec((1, ts, ATTN_WIDTH), tile),
            pl.BlockSpec((1, ts, d), tile),
            pl.BlockSpec((1, CONV_WIDTH), const),
            pl.BlockSpec((1, CONV_WIDTH), const),
            pl.BlockSpec((1, CONV_WIDTH), const),
            pl.BlockSpec((d, d), const, pipeline_mode=pl.Buffered(1)),
            pl.BlockSpec((1, d), const),
            pl.BlockSpec((1, d), const),
        ],
        out_specs=pl.BlockSpec((1, ts, d), tile),
        out_shape=jax.ShapeDtypeStruct((b, s, d), F32),
        compiler_params=pltpu.CompilerParams(
            dimension_semantics=("arbitrary", "arbitrary"), vmem_limit_bytes=VMEM_LIMIT),
        name="mix_out",
    )(conv, attn_n, h, ln_g, ln_b, out_g, w_out, g1, b1)


def _mem_attn_kernel(h_ref, wq_ref, k_ref, v_ref, wo_ref, g_ref, b_ref, o_ref, *, sub_rows):
    subs = [slice(st * sub_rows, (st + 1) * sub_rows) for st in range(h_ref.shape[1] // sub_rows)]
    heads = [slice(hd * MEM_HEAD_DIM, (hd + 1) * MEM_HEAD_DIM) for hd in range(MEM_HEADS)]
    scores = []
    for rows in subs:
        hb = h_ref[0, rows, :].astype(BF16)
        q = jnp.dot(hb, wq_ref[...], preferred_element_type=F32) * (MEM_HEAD_DIM ** -0.5)
        qb = q.astype(BF16)
        scores.append([lax.dot_general(qb[:, sl], k_ref[0, :, sl], (((1,), (1,)), ((), ())),
                                       preferred_element_type=F32) for sl in heads])
    atts = []
    for sc in scores:
        outs = []
        for s, sl in zip(sc, heads):
            m = jnp.max(s, axis=-1, keepdims=True)
            p = jnp.exp(s - m)
            l = jnp.sum(p, axis=-1, keepdims=True)
            o = jnp.dot(p.astype(BF16), v_ref[0, :, sl], preferred_element_type=F32) / l
            outs.append(o.astype(BF16))
        atts.append(jnp.dot(jnp.concatenate(outs, axis=1), wo_ref[...], preferred_element_type=F32))
    for rows, att in zip(subs, atts):
        o_ref[0, rows, :] = _layernorm(ALPHA * h_ref[0, rows, :] + att, g_ref[...], b_ref[...])


def _mem_attn(h1, wq, km, vm, wo, g2, b2, *, ts):
    b, s, d = h1.shape
    nm = km.shape[1]
    tile = lambda bi, ji: (bi, ji, 0)
    const = lambda bi, ji: (0, 0)
    per_b = lambda bi, ji: (bi, 0, 0)
    return pl.pallas_call(
        functools.partial(_mem_attn_kernel, sub_rows=SUB_ROWS),
        grid=(b, s // ts),
        in_specs=[
            pl.BlockSpec((1, ts, d), tile),
            pl.BlockSpec((d, d), const, pipeline_mode=pl.Buffered(1)),
            pl.BlockSpec((1, nm, d), per_b),
            pl.BlockSpec((1, nm, d), per_b),
            pl.BlockSpec((d, d), const, pipeline_mode=pl.Buffered(1)),
            pl.BlockSpec((1, d), const),
            pl.BlockSpec((1, d), const),
        ],
        out_specs=pl.BlockSpec((1, ts, d), tile),
        out_shape=jax.ShapeDtypeStruct((b, s, d), F32),
        compiler_params=pltpu.CompilerParams(
            dimension_semantics=("arbitrary", "arbitrary"), vmem_limit_bytes=VMEM_LIMIT),
        name="mem_attn",
    )(h1, wq, km, vm, wo, g2, b2)


def _mlp_kernel(h_ref, w1_ref, b1_ref, w2_ref, b2_ref, g_ref, b_ref, o_ref, *, ff_chunk, sub_rows):
    for st in range(h_ref.shape[0] // sub_rows):
        rows = slice(st * sub_rows, (st + 1) * sub_rows)
        h = h_ref[rows, :]
        hb = h.astype(BF16)
        acc = ALPHA * h + b2_ref[...]
        for f in range(D_FF // ff_chunk):
            sl = slice(f * ff_chunk, (f + 1) * ff_chunk)
            a = jnp.dot(hb, w1_ref[:, sl], preferred_element_type=F32) + b1_ref[:, sl]
            a = jnp.maximum(a, 0.0)
            acc = acc + jnp.dot((a * a).astype(BF16), w2_ref[sl, :], preferred_element_type=F32)
        o_ref[rows, :] = _layernorm(acc, g_ref[...], b_ref[...])


def _mlp(h2, w1, b1, w2, b2, g3, b3, *, tm):
    m, d = h2.shape
    row = lambda i: (i, 0)
    const = lambda i: (0, 0)
    kern = functools.partial(_mlp_kernel, ff_chunk=1024, sub_rows=SUB_ROWS)
    return pl.pallas_call(
        kern,
        grid=(m // tm,),
        in_specs=[
            pl.BlockSpec((tm, d), row),
            pl.BlockSpec((d, D_FF), const, pipeline_mode=pl.Buffered(1)),
            pl.BlockSpec((1, D_FF), const),
            pl.BlockSpec((D_FF, d), const, pipeline_mode=pl.Buffered(1)),
            pl.BlockSpec((1, d), const),
            pl.BlockSpec((1, d), const),
            pl.BlockSpec((1, d), const),
        ],
        out_specs=pl.BlockSpec((tm, d), row),
        out_shape=jax.ShapeDtypeStruct((m, d), F32),
        compiler_params=pltpu.CompilerParams(
            dimension_semantics=("arbitrary",), vmem_limit_bytes=VMEM_LIMIT),
        name="mlp",
    )(h2, w1, b1, w2, b2, g3, b3)


def _rope_tables(seq_len, gain, scale):
    rows = seq_len // GRID_W
    row_ids = jnp.repeat(jnp.arange(rows, dtype=jnp.int32), GRID_W)
    col_ids = jnp.tile(jnp.arange(GRID_W, dtype=jnp.int32), rows)
    inv = ROPE_THETA ** (-jnp.arange(0, AXIS_DIM, 2, dtype=jnp.float32) / AXIS_DIM)
    ang = jnp.concatenate([row_ids[:, None].astype(jnp.float32) * inv,
                           col_ids[:, None].astype(jnp.float32) * inv], axis=-1)
    cos = jnp.repeat(jnp.cos(ang), 2, axis=-1)
    sin = jnp.repeat(jnp.sin(ang), 2, axis=-1)
    sign = jnp.where(jnp.arange(HEAD_DIM) % 2 == 0, -1.0, 1.0).astype(jnp.float32)
    gain_sw = gain.reshape(HEAD_DIM // 2, 2)[:, ::-1].reshape(HEAD_DIM)
    c = cos * gain * scale
    s = sin * sign * gain_sw * scale
    return jnp.tile(c, (1, LANES // HEAD_DIM)), jnp.tile(s, (1, LANES // HEAD_DIM))


def kernel(x, mem, ln_in_g, ln_in_b, w_in, q_norm_g, k_norm_g, conv_w, conv_b, conv_ln_g, conv_ln_b,
           attn_out_g, conv_out_g, w_out, ln1_g, ln1_b, w_mem_q, w_mem_kv, w_mem_o, ln2_g, ln2_b,
           w_ff1, b_ff1, w_ff2, b_ff2, ln3_g, ln3_b):
    b, s, d = x.shape
    assert d == D_MODEL and w_in.shape[0] == DEPTH == 1
    assert s % GRID_W == 0
    m = b * s
    nm = mem.shape[1]
    row = lambda a: a.reshape(1, -1)

    cq, sq = _rope_tables(s, q_norm_g[0], HEAD_DIM ** -0.5 * LOG2_E)
    ck, sk = _rope_tables(s, k_norm_g[0], 1.0)
    seg = jnp.arange(LANES) // HEAD_DIM
    bd = (seg[:, None] == seg[None, :]).astype(BF16)

    h, q, kd, vd, u = _in_proj(x.reshape(m, d), row(ln_in_g), row(ln_in_b), w_in[0].astype(BF16),
                               cq, sq, ck, sk, bd, seq=s, tm=IN_PROJ_ROWS)
    km, vm = _mem_kv(mem.reshape(b * nm, d), w_mem_kv[0], tm=MEM_KV_ROWS)

    score_bound = (HEAD_DIM * jnp.max(jnp.abs(q_norm_g[0])) * jnp.max(jnp.abs(k_norm_g[0]))
                   * (HEAD_DIM ** -0.5 * LOG2_E))
    fast = (score_bound <= SAFE_EXP2_BOUND).astype(jnp.int32).reshape(1)

    attn_n, conv, (w_out_b, w_mq_b, w_mo_b, w_ff1_b, w_ff2_b) = _attention(
        fast, q.reshape(b, s, ATTN_WIDTH), kd.reshape(b, s, 2 * LANES), vd.reshape(b, s, 4 * LANES),
        row(attn_out_g[0]), u.reshape(b, s, CONV_WIDTH), conv_w[0], row(conv_b[0]),
        [w_out[0], w_mem_q[0], w_mem_o[0], w_ff1[0], w_ff2[0]], tq=ATTN_Q_ROWS)

    h1 = _mix_out(conv, attn_n, h.reshape(b, s, d), row(conv_ln_g[0]), row(conv_ln_b[0]),
                  row(conv_out_g[0]), w_out_b, row(ln1_g[0]), row(ln1_b[0]), ts=MIX_ROWS)

    h2 = _mem_attn(h1, w_mq_b, km.reshape(b, nm, d), vm.reshape(b, nm, d), w_mo_b,
                   row(ln2_g[0]), row(ln2_b[0]), ts=MEM_ATTN_ROWS)

    out = _mlp(h2.reshape(m, d), w_ff1_b, row(b_ff1[0]), w_ff2_b, row(b_ff2[0]),
               row(ln3_g[0]), row(ln3_b[0]), tm=MLP_ROWS)
    return out.reshape(b, s, d)
```

```python
import functools

import jax
import jax.numpy as jnp
from jax import lax
from jax.experimental import pallas as pl
from jax.experimental.pallas import tpu as pltpu

D_MODEL = 1024
HEAD_DIM = 64
ATTN_HEADS = 8
KV_HEADS = 2
ATTN_WIDTH = ATTN_HEADS * HEAD_DIM
KV_WIDTH = KV_HEADS * HEAD_DIM
CONV_WIDTH = D_MODEL - ATTN_WIDTH
CONV_K = 31
CONV_HALO = 16
MEM_HEADS = 4
MEM_HEAD_DIM = D_MODEL // MEM_HEADS
D_FF = 4 * D_MODEL
GRID_W = 64
AXIS_DIM = HEAD_DIM // 2
ROPE_THETA = 10000.0
EPS = 1e-5
DEPTH = 1
ALPHA = (2 * DEPTH) ** 0.25
LOG2_E = 1.4426950408889634

LANES = 128
F32_SUBLANES = 8
VMEM_LIMIT = 56 * 1024 * 1024

IN_PROJ_ROWS = 1024
MEM_KV_ROWS = 1024
ATTN_Q_ROWS = 2048
MIX_ROWS = 1024
MEM_ATTN_ROWS = 1024
MLP_ROWS = 1024
SUB_ROWS = 256
CONV_CHUNK_ROWS = 32
CONV_LAG_CHAINS = 4
SAFE_EXP2_BOUND = 60.0

F32 = jnp.float32
BF16 = jnp.bfloat16


def _layernorm(z, g, b):
    mu = jnp.mean(z, axis=-1, keepdims=True)
    zc = z - mu
    var = jnp.mean(zc * zc, axis=-1, keepdims=True)
    return zc * lax.rsqrt(var + EPS) * g + b


def _seg_mean_sq(z, bd):
    s = z * z
    hi = s.astype(BF16)
    lo = (s - hi.astype(F32)).astype(BF16)
    tot = (jnp.dot(hi, bd, preferred_element_type=F32)
           + jnp.dot(lo, bd, preferred_element_type=F32))
    return tot * (1.0 / HEAD_DIM)


def _in_proj_kernel(x_ref, g_ref, b_ref, w_ref, cq_ref, sq_ref, ck_ref, sk_ref, bd_ref,
                    h_ref, q_ref, kd_ref, vd_ref, u_ref, *, sub_rows):
    bd = bd_ref[...]
    lane = lax.broadcasted_iota(jnp.int32, (sub_rows, LANES), 1)
    even = (lane & 1) == 0
    lo = lane < HEAD_DIM
    ones = jnp.ones((sub_rows, LANES), BF16)
    c0 = ATTN_WIDTH + 2 * KV_WIDTH

    def norm_rope(z, c, s):
        sw = jnp.where(even, pltpu.roll(z, LANES - 1, 1), pltpu.roll(z, 1, 1))
        r = lax.rsqrt(_seg_mean_sq(z, bd) + EPS)
        return r * (z * c + sw * s)

    subs = [slice(st * sub_rows, (st + 1) * sub_rows) for st in range(x_ref.shape[0] // sub_rows)]
    hbs = []
    for rows in subs:
        h = _layernorm(x_ref[rows, :], g_ref[...], b_ref[...])
        h_ref[rows, :] = h
        hbs.append(h.astype(BF16))
    projs = []
    for hb in hbs:
        projs.append((
            jnp.dot(hb, w_ref[:, 0:ATTN_WIDTH], preferred_element_type=F32),
            jnp.dot(hb, w_ref[:, ATTN_WIDTH:c0], preferred_element_type=F32),
            jnp.dot(hb, w_ref[:, c0:c0 + CONV_WIDTH], preferred_element_type=F32),
            jnp.dot(hb, w_ref[:, c0 + CONV_WIDTH:c0 + 2 * CONV_WIDTH], preferred_element_type=F32)))
    for rows, (zq, zkv, val, gate) in zip(subs, projs):
        u_ref[rows, :] = val * jax.nn.sigmoid(gate)
        cq = cq_ref[rows, :]
        sq = sq_ref[rows, :]
        for g in range(ATTN_WIDTH // LANES):
            sl = slice(g * LANES, (g + 1) * LANES)
            q_ref[rows, sl] = norm_rope(zq[:, sl], cq, sq).astype(BF16)
        kk = norm_rope(zkv[:, 0:LANES], ck_ref[rows, :], sk_ref[rows, :])
        kr = pltpu.roll(kk, HEAD_DIM, 1)
        kd_ref[rows, 0:LANES] = jnp.where(lo, kk, kr).astype(BF16)
        kd_ref[rows, LANES:2 * LANES] = jnp.where(lo, kr, kk).astype(BF16)
        vv = zkv[:, LANES:2 * LANES]
        vr = pltpu.roll(vv, HEAD_DIM, 1)
        vd_ref[rows, 0:LANES] = jnp.where(lo, vv, vr).astype(BF16)
        vd_ref[rows, LANES:2 * LANES] = ones
        vd_ref[rows, 2 * LANES:3 * LANES] = jnp.where(lo, vr, vv).astype(BF16)
        vd_ref[rows, 3 * LANES:4 * LANES] = ones


def _in_proj(x2, ln_g, ln_b, w_in, cq, sq, ck, sk, bd, *, seq, tm):
    m, d = x2.shape
    n_in = w_in.shape[1]
    ns = seq // tm
    row = lambda i: (i, 0)
    const = lambda i: (0, 0)
    tab = lambda i: (i % ns, 0)
    return pl.pallas_call(
        functools.partial(_in_proj_kernel, sub_rows=SUB_ROWS),
        grid=(m // tm,),
        in_specs=[
            pl.BlockSpec((tm, d), row),
            pl.BlockSpec((1, d), const),
            pl.BlockSpec((1, d), const),
            pl.BlockSpec((d, n_in), const, pipeline_mode=pl.Buffered(1)),
            pl.BlockSpec((tm, LANES), tab),
            pl.BlockSpec((tm, LANES), tab),
            pl.BlockSpec((tm, LANES), tab),
            pl.BlockSpec((tm, LANES), tab),
            pl.BlockSpec((LANES, LANES), const),
        ],
        out_specs=[
            pl.BlockSpec((tm, d), row),
            pl.BlockSpec((tm, ATTN_WIDTH), row),
            pl.BlockSpec((tm, 2 * LANES), row),
            pl.BlockSpec((tm, 4 * LANES), row),
            pl.BlockSpec((tm, CONV_WIDTH), row),
        ],
        out_shape=[
            jax.ShapeDtypeStruct((m, d), F32),
            jax.ShapeDtypeStruct((m, ATTN_WIDTH), BF16),
            jax.ShapeDtypeStruct((m, 2 * LANES), BF16),
            jax.ShapeDtypeStruct((m, 4 * LANES), BF16),
            jax.ShapeDtypeStruct((m, CONV_WIDTH), F32),
        ],
        compiler_params=pltpu.CompilerParams(
            dimension_semantics=("arbitrary",), vmem_limit_bytes=VMEM_LIMIT),
        name="in_proj",
    )(x2, ln_g, ln_b, w_in, cq, sq, ck, sk, bd)


def _mem_kv_kernel(m_ref, w_ref, k_ref, v_ref):
    mb = m_ref[...].astype(BF16)
    wk = w_ref[:, 0:D_MODEL].astype(BF16)
    wv = w_ref[:, D_MODEL:2 * D_MODEL].astype(BF16)
    k_ref[...] = jnp.dot(mb, wk, preferred_element_type=F32).astype(BF16)
    v_ref[...] = jnp.dot(mb, wv, preferred_element_type=F32).astype(BF16)


def _mem_kv(mem2, w_kv, *, tm):
    m, d = mem2.shape
    row = lambda i: (i, 0)
    return pl.pallas_call(
        _mem_kv_kernel,
        grid=(m // tm,),
        in_specs=[pl.BlockSpec((tm, d), row),
                  pl.BlockSpec((d, 2 * d), lambda i: (0, 0), pipeline_mode=pl.Buffered(1))],
        out_specs=[pl.BlockSpec((tm, d), row), pl.BlockSpec((tm, d), row)],
        out_shape=[jax.ShapeDtypeStruct((m, d), BF16), jax.ShapeDtypeStruct((m, d), BF16)],
        compiler_params=pltpu.CompilerParams(
            dimension_semantics=("arbitrary",), vmem_limit_bytes=VMEM_LIMIT),
        name="mem_kv",
    )(mem2, w_kv)


def _conv_chunk(upad_ref, wb_ref, bias, row0, rc):
    sub = F32_SUBLANES
    wrows = rc + 2 * CONV_HALO
    off0 = CONV_HALO - CONV_K // 2
    win = upad_ref[pl.ds(row0, wrows), :]
    acc = [bias] * (rc // sub)
    for r in range(sub):
        sh = win if r == 0 else pltpu.roll(win, wrows - r, 0)
        for a in range((2 * CONV_HALO) // sub):
            t = sub * a + r - off0
            if 0 <= t < CONV_K:
                w = wb_ref[t]
                for i in range(rc // sub):
                    lo_row = sub * (a + i)
                    acc[i] = acc[i] + sh[lo_row:lo_row + sub, :] * w
    return acc


def _attention_kernel(q_ref, k_ref, v_ref, g_ref, u_ref, cw_ref, cb_ref, *rest,
                      sub_rows, rows_per_chunk, n_cast, conv_lag, shift_by_max):
    w_refs = rest[:n_cast]
    o_ref, c_ref = rest[n_cast:n_cast + 2]
    wb16_refs = rest[n_cast + 2:2 * n_cast + 2]
    upad_ref, wb_ref = rest[2 * n_cast + 2:]
    for w_ref, wb16_ref in zip(w_refs, wb16_refs):
        wb16_ref[...] = w_ref[...].astype(wb16_ref.dtype)
    tq = q_ref.shape[1]
    s = u_ref.shape[1]
    cw = u_ref.shape[2]
    j = pl.program_id(2)

    @pl.when(j == 0)
    def _():
        zeros = jnp.zeros((CONV_HALO, cw), F32)
        upad_ref[0:CONV_HALO, :] = zeros
        upad_ref[CONV_HALO + s:2 * CONV_HALO + s, :] = zeros
        upad_ref[CONV_HALO:CONV_HALO + s, :] = u_ref[0]
        for t in range(CONV_K):
            wb_ref[t] = jnp.broadcast_to(cw_ref[t:t + 1, :], (F32_SUBLANES, cw))

    q = q_ref[0]
    k = k_ref[0]
    v = v_ref[0]
    lane = lax.broadcasted_iota(jnp.int32, (sub_rows, LANES), 1)
    lo = lane < HEAD_DIM
    never = lane >= LANES + j
    bias = jnp.broadcast_to(cb_ref[...], (F32_SUBLANES, cw))
    rc = rows_per_chunk
    chains = [(rb, g, keep_lo) for rb in range(tq // sub_rows) for g in range(2)
              for keep_lo in (True, False)]
    n_chunks = tq // rc

    def conv_chunk(ci):
        r0 = ci * rc
        tiles = _conv_chunk(upad_ref, wb_ref, bias, pl.multiple_of(j * tq + r0, rc), rc)
        c_ref[0, r0:r0 + rc, :] = jnp.concatenate(tiles, axis=0)
        return tiles

    def zero_after(tiles):
        z = jnp.zeros((sub_rows, LANES), F32)
        if tiles:
            t = functools.reduce(lambda x, y: x + y, tiles)
            t = functools.reduce(lambda x, y: x + y,
                                 [t[:, c * LANES:(c + 1) * LANES] for c in range(cw // LANES)])
            z = jnp.where(never, jnp.broadcast_to(t[0:1, :], z.shape), z)
        return z.astype(q.dtype)

    def one_head(qg, keep_lo, zero, shift_by_max):
        lhs = jnp.where(lo, qg, zero) if keep_lo else jnp.where(lo, zero, qg)
        sc = lax.dot_general(lhs, k, (((1,), (1,)), ((), ())), preferred_element_type=F32)
        if shift_by_max:
            sc = sc - jnp.max(sc, axis=-1, keepdims=True)
        p = jnp.exp2(sc).astype(BF16)
        r = jnp.dot(p, v, preferred_element_type=F32)
        o = r[:, 0:LANES] / r[:, LANES:2 * LANES]
        return o * lax.rsqrt(jnp.mean(o * o, axis=-1, keepdims=True) + EPS)

    def attend(shift_by_max, groups):
        fed = []
        done = {}
        for ci, (rb, g, keep_lo) in enumerate(chains):
            fed.append([t for c in groups[ci] for t in conv_chunk(c)])
            rows = slice(rb * sub_rows, (rb + 1) * sub_rows)
            sl = slice(g * LANES, (g + 1) * LANES)
            zero = zero_after(fed[ci - conv_lag] if ci >= conv_lag else [])
            done[keep_lo] = one_head(q[rows, sl], keep_lo, zero, shift_by_max)
            if not keep_lo:
                og = jnp.where(lo, done[True], done[False])
                o_ref[0, rows, sl] = (og * g_ref[:, sl]).astype(o_ref.dtype)

    if shift_by_max:
        per_block = n_chunks // (tq // sub_rows)
        groups = [list(range(i // 4 * per_block, (i // 4 + 1) * per_block)) if i % 4 == 3 else []
                  for i in range(len(chains))]
    else:
        n_fed = len(chains) - conv_lag
        groups = [list(range((n_chunks * i) // n_fed, (n_chunks * (i + 1)) // n_fed))
                  if i < n_fed else [] for i in range(len(chains))]
    attend(shift_by_max, groups)


def _attention(q, kd, vd, gain, u, conv_w, conv_b, weights, *, tq, shift_by_max):
    b, s, _ = q.shape
    nj = s // tq
    n_steps = b * KV_HEADS * nj
    cw = CONV_WIDTH // KV_HEADS
    tile = lambda bi, hi, ji: (bi, ji, hi)
    per_bh = lambda bi, hi, ji: (bi, 0, hi)
    per_h = lambda bi, hi, ji: (0, hi)
    slab = lambda bi, hi, ji: ((bi * KV_HEADS + hi) * nj + ji, 0)
    w_specs = [pl.BlockSpec((w.shape[0] // n_steps, w.shape[1]), slab) for w in weights]
    kern = functools.partial(_attention_kernel, sub_rows=SUB_ROWS, rows_per_chunk=CONV_CHUNK_ROWS,
                             n_cast=len(weights), conv_lag=CONV_LAG_CHAINS,
                             shift_by_max=shift_by_max)
    outs = pl.pallas_call(
        kern,
        grid=(b, KV_HEADS, nj),
        in_specs=[
            pl.BlockSpec((1, tq, 2 * LANES), tile),
            pl.BlockSpec((1, s, LANES), per_bh),
            pl.BlockSpec((1, s, 2 * LANES), per_bh),
            pl.BlockSpec((1, 2 * LANES), per_h),
            pl.BlockSpec((1, s, cw), per_bh),
            pl.BlockSpec((CONV_K, cw), per_h),
            pl.BlockSpec((1, cw), per_h),
        ] + w_specs,
        out_specs=[
            pl.BlockSpec((1, tq, 2 * LANES), tile),
            pl.BlockSpec((1, tq, cw), tile),
        ] + w_specs,
        out_shape=[
            jax.ShapeDtypeStruct((b, s, ATTN_WIDTH), BF16),
            jax.ShapeDtypeStruct((b, s, CONV_WIDTH), F32),
        ] + [jax.ShapeDtypeStruct(w.shape, BF16) for w in weights],
        scratch_shapes=[
            pltpu.VMEM((s + 2 * CONV_HALO, cw), F32),
            pltpu.VMEM((CONV_K, F32_SUBLANES, cw), F32),
        ],
        compiler_params=pltpu.CompilerParams(
            dimension_semantics=("arbitrary", "arbitrary", "arbitrary"),
            vmem_limit_bytes=VMEM_LIMIT),
        name="attention",
    )(q, kd, vd, gain, u, conv_w, conv_b, *weights)
    return outs[0], outs[1], tuple(outs[2:])


def _seg_mean_sq_lanes(z):
    lo = lax.broadcasted_iota(jnp.int32, z.shape, 1) < HEAD_DIM
    s = z * z
    s_lo = jnp.sum(jnp.where(lo, s, 0.0), axis=-1, keepdims=True)
    s_hi = jnp.sum(jnp.where(lo, 0.0, s), axis=-1, keepdims=True)
    return jnp.where(lo, s_lo, s_hi) * (1.0 / HEAD_DIM)


def _mix_out_kernel(c_ref, a_ref, h_ref, lg_ref, lb_ref, cog_ref,
                    wo_ref, g1_ref, b1_ref, o_ref, *, sub_rows):
    subs =[slice(st * sub_rows, (st + 1) * sub_rows) for st in range(a_ref.shape[1] // sub_rows)]
    ycs = []
    for rows in subs:
        c = _layernorm(c_ref[0, rows, :], lg_ref[...], lb_ref[...])
        c = c * jax.nn.sigmoid(c)
        parts = []
        for g in range(CONV_WIDTH // LANES):
            sl = slice(g * LANES, (g + 1) * LANES)
            cg = c[:, sl]
            parts.append((cg * lax.rsqrt(_seg_mean_sq_lanes(cg) + EPS) * cog_ref[:, sl]).astype(BF16))
        ycs.append(jnp.concatenate(parts, axis=1))
    mixes = []
    for rows, yc in zip(subs, ycs):
        mixes.append(jnp.dot(a_ref[0, rows, :], wo_ref[0:ATTN_WIDTH, :], preferred_element_type=F32)
                     + jnp.dot(yc, wo_ref[ATTN_WIDTH:D_MODEL, :], preferred_element_type=F32))
    for rows, mix in zip(subs, mixes):
        o_ref[0, rows, :] = _layernorm(ALPHA * h_ref[0, rows, :] + mix, g1_ref[...], b1_ref[...])


def _mix_out(conv, attn_n, h, ln_g, ln_b, out_g, w_out, g1, b1, *, ts):
    b, s, d = h.shape
    tile = lambda bi, ji: (bi, ji, 0)
    const = lambda bi, ji: (0, 0)
    return pl.pallas_call(
        functools.partial(_mix_out_kernel, sub_rows=SUB_ROWS),
        grid=(b, s // ts),
        in_specs=[
            pl.BlockSpec((1, ts, CONV_WIDTH), tile),
            pl.BlockSpec((1, ts, ATTN_WIDTH), tile),
            pl.BlockSpec((1, ts, d), tile),
            pl.BlockSpec((1, CONV_WIDTH), const),
            pl.BlockSpec((1, CONV_WIDTH), const),
            pl.BlockSpec((1, CONV_WIDTH), const),
            pl.BlockSpec((d, d), const, pipeline_mode=pl.Buffered(1)),
            pl.BlockSpec((1, d), const),
            pl.BlockSpec((1, d), const),
        ],
        out_specs=pl.BlockSpec((1, ts, d), tile),
        out_shape=jax.ShapeDtypeStruct((b, s, d), F32),
        compiler_params=pltpu.CompilerParams(
            dimension_semantics=("arbitrary", "arbitrary"), vmem_limit_bytes=VMEM_LIMIT),
        name="mix_out",
    )(conv, attn_n, h, ln_g, ln_b, out_g, w_out, g1, b1)


def _mem_attn_kernel(h_ref, wq_ref, k_ref, v_ref, wo_ref, g_ref, b_ref, o_ref, *, sub_rows):
    subs = [slice(st * sub_rows, (st + 1) * sub_rows) for st in range(h_ref.shape[1] // sub_rows)]
    heads = [slice(hd * MEM_HEAD_DIM, (hd + 1) * MEM_HEAD_DIM) for hd in range(MEM_HEADS)]
    scores = []
    for rows in subs:
        hb = h_ref[0, rows, :].astype(BF16)
        q = jnp.dot(hb, wq_ref[...], preferred_element_type=F32) * (MEM_HEAD_DIM ** -0.5)
        qb = q.astype(BF16)
        scores.append([lax.dot_general(qb[:, sl], k_ref[0, :, sl], (((1,), (1,)), ((), ())),
                                       preferred_element_type=F32) for sl in heads])
    atts = []
    for sc in scores:
        outs = []
        for s, sl in zip(sc, heads):
            m = jnp.max(s, axis=-1, keepdims=True)
            p = jnp.exp(s - m)
            l = jnp.sum(p, axis=-1, keepdims=True)
            o = jnp.dot(p.astype(BF16), v_ref[0, :, sl], preferred_element_type=F32) / l
            outs.append(o.astype(BF16))
        atts.append(jnp.dot(jnp.concatenate(outs, axis=1), wo_ref[...], preferred_element_type=F32))
    for rows, att in zip(subs, atts):
        o_ref[0, rows, :] = _layernorm(ALPHA * h_ref[0, rows, :] + att, g_ref[...], b_ref[...])


def _mem_attn(h1, wq, km, vm, wo, g2, b2, *, ts):
    b, s, d = h1.shape
    nm = km.shape[1]
    tile = lambda bi, ji: (bi, ji, 0)
    const = lambda bi, ji: (0, 0)
    per_b = lambda bi, ji: (bi, 0, 0)
    return pl.pallas_call(
        functools.partial(_mem_attn_kernel, sub_rows=SUB_ROWS),
        grid=(b, s // ts),
        in_specs=[
            pl.BlockSpec((1, ts, d), tile),
            pl.BlockSpec((d, d), const, pipeline_mode=pl.Buffered(1)),
            pl.BlockSpec((1, nm, d), per_b),
            pl.BlockSpec((1, nm, d), per_b),
            pl.BlockSpec((d, d), const, pipeline_mode=pl.Buffered(1)),
            pl.BlockSpec((1, d), const),
            pl.BlockSpec((1, d), const),
        ],
        out_specs=pl.BlockSpec((1, ts, d), tile),
        out_shape=jax.ShapeDtypeStruct((b, s, d), F32),
        compiler_params=pltpu.CompilerParams(
            dimension_semantics=("arbitrary", "arbitrary"), vmem_limit_bytes=VMEM_LIMIT),
        name="mem_attn",
    )(h1, wq, km, vm, wo, g2, b2)


def _mlp_kernel(h_ref, w1_ref, b1_ref, w2_ref, b2_ref, g_ref, b_ref, o_ref, *, ff_chunk, sub_rows):
    for st in range(h_ref.shape[0] // sub_rows):
        rows = slice(st * sub_rows, (st + 1) * sub_rows)
        h = h_ref[rows, :]
        hb = h.astype(BF16)
        acc = ALPHA * h + b2_ref[...]
        for f in range(D_FF // ff_chunk):
            sl = slice(f * ff_chunk, (f + 1) * ff_chunk)
            a = jnp.dot(hb, w1_ref[:, sl], preferred_element_type=F32) + b1_ref[:, sl]
            a = jnp.maximum(a, 0.0)
            acc = acc + jnp.dot((a * a).astype(BF16), w2_ref[sl, :], preferred_element_type=F32)
        o_ref[rows, :] = _layernorm(acc, g_ref[...], b_ref[...])


def _mlp(h2, w1, b1, w2, b2, g3, b3, *, tm):
    m, d = h2.shape
    row = lambda i: (i, 0)
    const = lambda i: (0, 0)
    kern = functools.partial(_mlp_kernel, ff_chunk=1024, sub_rows=SUB_ROWS)
    return pl.pallas_call(
        kern,
        grid=(m // tm,),
        in_specs=[
            pl.BlockSpec((tm, d), row),
            pl.BlockSpec((d, D_FF), const, pipeline_mode=pl.Buffered(1)),
            pl.BlockSpec((1, D_FF), const),
            pl.BlockSpec((D_FF, d), const, pipeline_mode=pl.Buffered(1)),
            pl.BlockSpec((1, d), const),
            pl.BlockSpec((1, d), const),
            pl.BlockSpec((1, d), const),
        ],
        out_specs=pl.BlockSpec((tm, d), row),
        out_shape=jax.ShapeDtypeStruct((m, d), F32),
        compiler_params=pltpu.CompilerParams(
            dimension_semantics=("arbitrary",), vmem_limit_bytes=VMEM_LIMIT),
        name="mlp",
    )(h2, w1, b1, w2, b2, g3, b3)


def _rope_tables(seq_len, gain, scale):
    rows = seq_len // GRID_W
    row_ids = jnp.repeat(jnp.arange(rows, dtype=jnp.int32), GRID_W)
    col_ids = jnp.tile(jnp.arange(GRID_W, dtype=jnp.int32), rows)
    inv = ROPE_THETA ** (-jnp.arange(0, AXIS_DIM, 2, dtype=jnp.float32) / AXIS_DIM)
    ang = jnp.concatenate([row_ids[:, None].astype(jnp.float32) * inv,
                           col_ids[:, None].astype(jnp.float32) * inv], axis=-1)
    cos = jnp.repeat(jnp.cos(ang), 2, axis=-1)
    sin = jnp.repeat(jnp.sin(ang), 2, axis=-1)
    sign = jnp.where(jnp.arange(HEAD_DIM) % 2 == 0, -1.0, 1.0).astype(jnp.float32)
    gain_sw = gain.reshape(HEAD_DIM // 2, 2)[:, ::-1].reshape(HEAD_DIM)
    c = cos * gain * scale
    s = sin * sign * gain_sw * scale
    return jnp.tile(c, (1, LANES // HEAD_DIM)), jnp.tile(s, (1, LANES // HEAD_DIM))


def kernel(x, mem, ln_in_g, ln_in_b, w_in, q_norm_g, k_norm_g, conv_w, conv_b, conv_ln_g, conv_ln_b,
           attn_out_g, conv_out_g, w_out, ln1_g, ln1_b, w_mem_q, w_mem_kv, w_mem_o, ln2_g, ln2_b,
           w_ff1, b_ff1, w_ff2, b_ff2, ln3_g, ln3_b):
    b, s, d = x.shape
    assert d == D_MODEL and w_in.shape[0] == DEPTH == 1
    assert s % GRID_W == 0
    m = b * s
    nm = mem.shape[1]
    row = lambda a: a.reshape(1, -1)

    cq, sq = _rope_tables(s, q_norm_g[0], HEAD_DIM ** -0.5 * LOG2_E)
    ck, sk = _rope_tables(s, k_norm_g[0], 1.0)
    seg = jnp.arange(LANES) // HEAD_DIM
    bd = (seg[:, None] == seg[None, :]).astype(BF16)

    h, q, kd, vd, u = _in_proj(x.reshape(m, d), row(ln_in_g), row(ln_in_b), w_in[0].astype(BF16),
                               cq, sq, ck, sk, bd, seq=s, tm=IN_PROJ_ROWS)
    km, vm = _mem_kv(mem.reshape(b * nm, d), w_mem_kv[0], tm=MEM_KV_ROWS)

    score_bound = (HEAD_DIM * jnp.max(jnp.abs(q_norm_g[0])) * jnp.max(jnp.abs(k_norm_g[0]))
                   * (HEAD_DIM ** -0.5 * LOG2_E))
    attn_n, conv, (w_out_b, w_mq_b, w_mo_b, w_ff1_b, w_ff2_b) = lax.cond(
        score_bound <= SAFE_EXP2_BOUND,
        functools.partial(_attention, tq=ATTN_Q_ROWS, shift_by_max=False),
        functools.partial(_attention, tq=ATTN_Q_ROWS, shift_by_max=True),
        q.reshape(b, s, ATTN_WIDTH), kd.reshape(b, s, 2 * LANES), vd.reshape(b, s, 4 * LANES),
        row(attn_out_g[0]), u.reshape(b, s, CONV_WIDTH), conv_w[0], row(conv_b[0]),
        (w_out[0], w_mem_q[0], w_mem_o[0], w_ff1[0], w_ff2[0]))

    h1 = _mix_out(conv, attn_n, h.reshape(b, s, d), row(conv_ln_g[0]), row(conv_ln_b[0]),
                  row(conv_out_g[0]), w_out_b, row(ln1_g[0]), row(ln1_b[0]), ts=MIX_ROWS)

    h2 = _mem_attn(h1, w_mq_b, km.reshape(b, nm, d), vm.reshape(b, nm, d), w_mo_b,
                   row(ln2_g[0]), row(ln2_b[0]), ts=MEM_ATTN_ROWS)

    out = _mlp(h2.reshape(m, d), w_ff1_b, row(b_ff1[0]), w_ff2_b, row(b_ff2[0]),
               row(ln3_g[0]), row(ln3_b[0]), tm=MLP_ROWS)
    return out.reshape(b, s, d)
```

```python
import functools

import jax
import jax.numpy as jnp
from jax import lax
from jax.experimental import pallas as pl
from jax.experimental.pallas import tpu as pltpu

D_MODEL = 1024
HEAD_DIM = 64
ATTN_HEADS = 8
KV_HEADS = 2
ATTN_WIDTH = ATTN_HEADS * HEAD_DIM
KV_WIDTH = KV_HEADS * HEAD_DIM
CONV_WIDTH = D_MODEL - ATTN_WIDTH
CONV_K = 31
CONV_HALO = 16
MEM_HEADS = 4
MEM_HEAD_DIM = D_MODEL // MEM_HEADS
D_FF = 4 * D_MODEL
GRID_W = 64
AXIS_DIM = HEAD_DIM // 2
ROPE_THETA = 10000.0
EPS = 1e-5
DEPTH = 1
ALPHA = (2 * DEPTH) ** 0.25
LOG2_E = 1.4426950408889634

LANES = 128
F32_SUBLANES = 8
VMEM_LIMIT = 56 * 1024 * 1024

IN_PROJ_ROWS = 1024
MEM_KV_ROWS = 1024
ATTN_Q_ROWS = 1024
MIX_ROWS = 1024
MEM_ATTN_ROWS = 1024
MLP_ROWS = 1024
SUB_ROWS = 256
CONV_CHUNK_ROWS = 32
CONV_LAG_CHAINS = 4
SAFE_EXP2_BOUND = 60.0

F32 = jnp.float32
BF16 = jnp.bfloat16


def _layernorm(z, g, b):
    mu = jnp.mean(z, axis=-1, keepdims=True)
    zc = z - mu
    var = jnp.mean(zc * zc, axis=-1, keepdims=True)
    return zc * lax.rsqrt(var + EPS) * g + b


def _seg_mean_sq(z, bd):
    s = z * z
    hi = s.astype(BF16)
    lo = (s - hi.astype(F32)).astype(BF16)
    tot = (jnp.dot(hi, bd, preferred_element_type=F32)
           + jnp.dot(lo, bd, preferred_element_type=F32))
    return tot * (1.0 / HEAD_DIM)


def _in_proj_kernel(x_ref, g_ref, b_ref, w_ref, cq_ref, sq_ref, ck_ref, sk_ref, bd_ref,
                    h_ref, q_ref, kd_ref, vd_ref, u_ref, *, sub_rows):
    bd = bd_ref[...]
    lane = lax.broadcasted_iota(jnp.int32, (sub_rows, LANES), 1)
    even = (lane & 1) == 0
    lo = lane < HEAD_DIM
    ones = jnp.ones((sub_rows, LANES), BF16)
    c0 = ATTN_WIDTH + 2 * KV_WIDTH

    def norm_rope(z, c, s):
        sw = jnp.where(even, pltpu.roll(z, LANES - 1, 1), pltpu.roll(z, 1, 1))
        r = lax.rsqrt(_seg_mean_sq(z, bd) + EPS)
        return r * (z * c + sw * s)

    subs = [slice(st * sub_rows, (st + 1) * sub_rows) for st in range(x_ref.shape[0] // sub_rows)]
    hbs = []
    for rows in subs:
        h = _layernorm(x_ref[rows, :], g_ref[...], b_ref[...])
        h_ref[rows, :] = h
        hbs.append(h.astype(BF16))
    projs = []
    for hb in hbs:
        projs.append((
            jnp.dot(hb, w_ref[:, 0:ATTN_WIDTH], preferred_element_type=F32),
            jnp.dot(hb, w_ref[:, ATTN_WIDTH:c0], preferred_element_type=F32),
            jnp.dot(hb, w_ref[:, c0:c0 + CONV_WIDTH], preferred_element_type=F32),
            jnp.dot(hb, w_ref[:, c0 + CONV_WIDTH:c0 + 2 * CONV_WIDTH], preferred_element_type=F32)))
    for rows, (zq, zkv, val, gate) in zip(subs, projs):
        u_ref[rows, :] = val * jax.nn.sigmoid(gate)
        cq = cq_ref[rows, :]
        sq = sq_ref[rows, :]
        for g in range(ATTN_WIDTH // LANES):
            sl = slice(g * LANES, (g + 1) * LANES)
            q_ref[rows, sl] = norm_rope(zq[:, sl], cq, sq).astype(BF16)
        kk = norm_rope(zkv[:, 0:LANES], ck_ref[rows, :], sk_ref[rows, :])
        kr = pltpu.roll(kk, HEAD_DIM, 1)
        kd_ref[rows, 0:LANES] = jnp.where(lo, kk, kr).astype(BF16)
        kd_ref[rows, LANES:2 * LANES] = jnp.where(lo, kr, kk).astype(BF16)
        vv = zkv[:, LANES:2 * LANES]
        vr = pltpu.roll(vv, HEAD_DIM, 1)
        vd_ref[rows, 0:LANES] = jnp.where(lo, vv, vr).astype(BF16)
        vd_ref[rows, LANES:2 * LANES] = ones
        vd_ref[rows, 2 * LANES:3 * LANES] = jnp.where(lo, vr, vv).astype(BF16)
        vd_ref[rows, 3 * LANES:4 * LANES] = ones


def _in_proj(x2, ln_g, ln_b, w_in, cq, sq, ck, sk, bd, *, seq, tm):
    m, d = x2.shape
    n_in = w_in.shape[1]
    ns = seq // tm
    row = lambda i: (i, 0)
    const = lambda i: (0, 0)
    tab = lambda i: (i % ns, 0)
    return pl.pallas_call(
        functools.partial(_in_proj_kernel, sub_rows=SUB_ROWS),
        grid=(m // tm,),
        in_specs=[
            pl.BlockSpec((tm, d), row),
            pl.BlockSpec((1, d), const),
            pl.BlockSpec((1, d), const),
            pl.BlockSpec((d, n_in), const, pipeline_mode=pl.Buffered(1)),
            pl.BlockSpec((tm, LANES), tab),
            pl.BlockSpec((tm, LANES), tab),
            pl.BlockSpec((tm, LANES), tab),
            pl.BlockSpec((tm, LANES), tab),
            pl.BlockSpec((LANES, LANES), const),
        ],
        out_specs=[
            pl.BlockSpec((tm, d), row),
            pl.BlockSpec((tm, ATTN_WIDTH), row),
            pl.BlockSpec((tm, 2 * LANES), row),
            pl.BlockSpec((tm, 4 * LANES), row),
            pl.BlockSpec((tm, CONV_WIDTH), row),
        ],
        out_shape=[
            jax.ShapeDtypeStruct((m, d), F32),
            jax.ShapeDtypeStruct((m, ATTN_WIDTH), BF16),
            jax.ShapeDtypeStruct((m, 2 * LANES), BF16),
            jax.ShapeDtypeStruct((m, 4 * LANES), BF16),
            jax.ShapeDtypeStruct((m, CONV_WIDTH), F32),
        ],
        compiler_params=pltpu.CompilerParams(
            dimension_semantics=("arbitrary",), vmem_limit_bytes=VMEM_LIMIT),
        name="in_proj",
    )(x2, ln_g, ln_b, w_in, cq, sq, ck, sk, bd)


def _mem_kv_kernel(m_ref, w_ref, k_ref, v_ref):
    mb = m_ref[...].astype(BF16)
    wk = w_ref[:, 0:D_MODEL].astype(BF16)
    wv = w_ref[:, D_MODEL:2 * D_MODEL].astype(BF16)
    k_ref[...] = jnp.dot(mb, wk, preferred_element_type=F32).astype(BF16)
    v_ref[...] = jnp.dot(mb, wv, preferred_element_type=F32).astype(BF16)


def _mem_kv(mem2, w_kv, *, tm):
    m, d = mem2.shape
    row = lambda i: (i, 0)
    return pl.pallas_call(
        _mem_kv_kernel,
        grid=(m // tm,),
        in_specs=[pl.BlockSpec((tm, d), row),
                  pl.BlockSpec((d, 2 * d), lambda i: (0, 0), pipeline_mode=pl.Buffered(1))],
        out_specs=[pl.BlockSpec((tm, d), row), pl.BlockSpec((tm, d), row)],
        out_shape=[jax.ShapeDtypeStruct((m, d), BF16), jax.ShapeDtypeStruct((m, d), BF16)],
        compiler_params=pltpu.CompilerParams(
            dimension_semantics=("arbitrary",), vmem_limit_bytes=VMEM_LIMIT),
        name="mem_kv",
    )(mem2, w_kv)


def _conv_chunk(upad_ref, wb_ref, bias, row0, rc):
    sub = F32_SUBLANES
    wrows = rc + 2 * CONV_HALO
    off0 = CONV_HALO - CONV_K // 2
    win = upad_ref[pl.ds(row0, wrows), :]
    acc = [bias] * (rc // sub)
    for r in range(sub):
        sh = win if r == 0 else pltpu.roll(win, wrows - r, 0)
        for a in range((2 * CONV_HALO) // sub):
            t = sub * a + r - off0
            if 0 <= t < CONV_K:
                w = wb_ref[t]
                for i in range(rc // sub):
                    lo_row = sub * (a + i)
                    acc[i] = acc[i] + sh[lo_row:lo_row + sub, :] * w
    return acc


def _attention_kernel(fast_ref, q_ref, k_ref, v_ref, g_ref, u_ref, cw_ref, cb_ref, *rest,
                      sub_rows, rows_per_chunk, n_cast, conv_lag):
    w_refs = rest[:n_cast]
    o_ref, c_ref = rest[n_cast:n_cast + 2]
    wb16_refs = rest[n_cast + 2:2 * n_cast + 2]
    upad_ref, wb_ref = rest[2 * n_cast + 2:]
    for w_ref, wb16_ref in zip(w_refs, wb16_refs):
        wb16_ref[...] = w_ref[...].astype(wb16_ref.dtype)
    tq = q_ref.shape[1]
    s = u_ref.shape[1]
    cw = u_ref.shape[2]
    j = pl.program_id(2)

    @pl.when(j == 0)
    def _():
        zeros = jnp.zeros((CONV_HALO, cw), F32)
        upad_ref[0:CONV_HALO, :] = zeros
        upad_ref[CONV_HALO + s:2 * CONV_HALO + s, :] = zeros
        upad_ref[CONV_HALO:CONV_HALO + s, :] = u_ref[0]
        for t in range(CONV_K):
            wb_ref[t] = jnp.broadcast_to(cw_ref[t:t + 1, :], (F32_SUBLANES, cw))

    q = q_ref[0]
    k = k_ref[0]
    v = v_ref[0]
    lane = lax.broadcasted_iota(jnp.int32, (sub_rows, LANES), 1)
    lo = lane < HEAD_DIM
    never = lane >= LANES + j
    bias = jnp.broadcast_to(cb_ref[...], (F32_SUBLANES, cw))
    rc = rows_per_chunk
    chains = [(rb, g, keep_lo) for rb in range(tq // sub_rows) for g in range(2)
              for keep_lo in (True, False)]
    n_chunks = tq // rc

    def conv_chunk(ci):
        r0 = ci * rc
        tiles = _conv_chunk(upad_ref, wb_ref, bias, pl.multiple_of(j * tq + r0, rc), rc)
        c_ref[0, r0:r0 + rc, :] = jnp.concatenate(tiles, axis=0)
        return tiles

    def zero_after(tiles):
        z = jnp.zeros((sub_rows, LANES), F32)
        if tiles:
            t = functools.reduce(lambda x, y: x + y, tiles)
            t = functools.reduce(lambda x, y: x + y,
                                 [t[:, c * LANES:(c + 1) * LANES] for c in range(cw // LANES)])
            z = jnp.where(never, jnp.broadcast_to(t[0:1, :], z.shape), z)
        return z.astype(q.dtype)

    def one_head(qg, keep_lo, zero, shift_by_max):
        lhs = jnp.where(lo, qg, zero) if keep_lo else jnp.where(lo, zero, qg)
        sc = lax.dot_general(lhs, k, (((1,), (1,)), ((), ())), preferred_element_type=F32)
        if shift_by_max:
            sc = sc - jnp.max(sc, axis=-1, keepdims=True)
        p = jnp.exp2(sc).astype(BF16)
        r = jnp.dot(p, v, preferred_element_type=F32)
        o = r[:, 0:LANES] / r[:, LANES:2 * LANES]
        return o * lax.rsqrt(jnp.mean(o * o, axis=-1, keepdims=True) + EPS)

    def attend(shift_by_max, groups):
        fed = []
        done = {}
        for ci, (rb, g, keep_lo) in enumerate(chains):
            fed.append([t for c in groups[ci] for t in conv_chunk(c)])
            rows = slice(rb * sub_rows, (rb + 1) * sub_rows)
            sl = slice(g * LANES, (g + 1) * LANES)
            zero = zero_after(fed[ci - conv_lag] if ci >= conv_lag else [])
            done[keep_lo] = one_head(q[rows, sl], keep_lo, zero, shift_by_max)
            if not keep_lo:
                og = jnp.where(lo, done[True], done[False])
                o_ref[0, rows, sl] = (og * g_ref[:, sl]).astype(o_ref.dtype)

    fast = fast_ref[0] > 0
    n_fed = len(chains) - conv_lag
    spread = [list(range((n_chunks * i) // n_fed, (n_chunks * (i + 1)) // n_fed)) if i < n_fed
              else [] for i in range(len(chains))]
    per_block = n_chunks // (tq // sub_rows)
    blocked = [list(range(i // 4 * per_block, (i // 4 + 1) * per_block)) if i % 4 == 3 else []
               for i in range(len(chains))]

    @pl.when(fast)
    def _():
        attend(False, spread)

    @pl.when(jnp.logical_not(fast))
    def _():
        attend(True, blocked)


def _attention(fast, q, kd, vd, gain, u, conv_w, conv_b, weights, *, tq):
    b, s, _ = q.shape
    nj = s // tq
    n_steps = b * KV_HEADS * nj
    cw = CONV_WIDTH // KV_HEADS
    tile = lambda bi, hi, ji: (bi, ji, hi)
    per_bh = lambda bi, hi, ji: (bi, 0, hi)
    per_h = lambda bi, hi, ji: (0, hi)
    slab = lambda bi, hi, ji: ((bi * KV_HEADS + hi) * nj + ji, 0)
    w_specs = [pl.BlockSpec((w.shape[0] // n_steps, w.shape[1]), slab) for w in weights]
    kern = functools.partial(_attention_kernel, sub_rows=SUB_ROWS, rows_per_chunk=CONV_CHUNK_ROWS,
                             n_cast=len(weights), conv_lag=CONV_LAG_CHAINS)
    outs = pl.pallas_call(
        kern,
        grid=(b, KV_HEADS, nj),
        in_specs=[
            pl.BlockSpec(memory_space=pltpu.SMEM),
            pl.BlockSpec((1, tq, 2 * LANES), tile),
            pl.BlockSpec((1, s, LANES), per_bh),
            pl.BlockSpec((1, s, 2 * LANES), per_bh),
            pl.BlockSpec((1, 2 * LANES), per_h),
            pl.BlockSpec((1, s, cw), per_bh),
            pl.BlockSpec((CONV_K, cw), per_h),
            pl.BlockSpec((1, cw), per_h),
        ] + w_specs,
        out_specs=[
            pl.BlockSpec((1, tq, 2 * LANES), tile),
            pl.BlockSpec((1, tq, cw), tile),
        ] + w_specs,
        out_shape=[
            jax.ShapeDtypeStruct((b, s, ATTN_WIDTH), BF16),
            jax.ShapeDtypeStruct((b, s, CONV_WIDTH), F32),
        ] + [jax.ShapeDtypeStruct(w.shape, BF16) for w in weights],
        scratch_shapes=[
            pltpu.VMEM((s + 2 * CONV_HALO, cw), F32),
            pltpu.VMEM((CONV_K, F32_SUBLANES, cw), F32),
        ],
        compiler_params=pltpu.CompilerParams(
            dimension_semantics=("arbitrary", "arbitrary", "arbitrary"),
            vmem_limit_bytes=VMEM_LIMIT),
        name="attention",
    )(fast, q, kd, vd, gain, u, conv_w, conv_b, *weights)
    return outs[0], outs[1], outs[2:]


def _seg_mean_sq_lanes(z):
    lo = lax.broadcasted_iota(jnp.int32, z.shape, 1) < HEAD_DIM
    s = z * z
    s_lo = jnp.sum(jnp.where(lo, s, 0.0), axis=-1, keepdims=True)
    s_hi = jnp.sum(jnp.where(lo, 0.0, s), axis=-1, keepdims=True)
    return jnp.where(lo, s_lo, s_hi) * (1.0 / HEAD_DIM)


def _mix_out_kernel(c_ref, a_ref, h_ref, lg_ref, lb_ref, cog_ref,
                    wo_ref, g1_ref, b1_ref, o_ref, *, sub_rows):
    subs =[slice(st * sub_rows, (st + 1) * sub_rows) for st in range(a_ref.shape[1] // sub_rows)]
    ycs = []
    for rows in subs:
        c = _layernorm(c_ref[0, rows, :], lg_ref[...], lb_ref[...])
        c = c * jax.nn.sigmoid(c)
        parts = []
        for g in range(CONV_WIDTH // LANES):
            sl = slice(g * LANES, (g + 1) * LANES)
            cg = c[:, sl]
            parts.append((cg * lax.rsqrt(_seg_mean_sq_lanes(cg) + EPS) * cog_ref[:, sl]).astype(BF16))
        ycs.append(jnp.concatenate(parts, axis=1))
    mixes = []
    for rows, yc in zip(subs, ycs):
        mixes.append(jnp.dot(a_ref[0, rows, :], wo_ref[0:ATTN_WIDTH, :], preferred_element_type=F32)
                     + jnp.dot(yc, wo_ref[ATTN_WIDTH:D_MODEL, :], preferred_element_type=F32))
    for rows, mix in zip(subs, mixes):
        o_ref[0, rows, :] = _layernorm(ALPHA * h_ref[0, rows, :] + mix, g1_ref[...], b1_ref[...])


def _mix_out(conv, attn_n, h, ln_g, ln_b, out_g, w_out, g1, b1, *, ts):
    b, s, d = h.shape
    tile = lambda bi, ji: (bi, ji, 0)
    const = lambda bi, ji: (0, 0)
    return pl.pallas_call(
        functools.partial(_mix_out_kernel, sub_rows=SUB_ROWS),
        grid=(b, s // ts),
        in_specs=[
            pl.BlockSpec((1, ts, CONV_WIDTH), tile),
            pl.BlockSpec((1, ts, ATTN_WIDTH), tile),
            pl.BlockSpec((1, ts, d), tile),
            pl.BlockSpec((1, CONV_WIDTH), const),
            pl.BlockSpec((1, CONV_WIDTH), const),
            pl.BlockSpec((1, CONV_WIDTH), const),
            pl.BlockSpec((d, d), const, pipeline_mode=pl.Buffered(1)),
            pl.BlockSpec((1, d), const),
            pl.BlockSpec((1, d), const),
        ],
        out_specs=pl.BlockSpec((1, ts, d), tile),
        out_shape=jax.ShapeDtypeStruct((b, s, d), F32),
        compiler_params=pltpu.CompilerParams(
            dimension_semantics=("arbitrary", "arbitrary"), vmem_limit_bytes=VMEM_LIMIT),
        name="mix_out",
    )(conv, attn_n, h, ln_g, ln_b, out_g, w_out, g1, b1)


def _mem_attn_kernel(h_ref, wq_ref, k_ref, v_ref, wo_ref, g_ref, b_ref, o_ref, *, sub_rows):
    subs = [slice(st * sub_rows, (st + 1) * sub_rows) for st in range(h_ref.shape[1] // sub_rows)]
    heads = [slice(hd * MEM_HEAD_DIM, (hd + 1) * MEM_HEAD_DIM) for hd in range(MEM_HEADS)]
    scores = []
    for rows in subs:
        hb = h_ref[0, rows, :].astype(BF16)
        q = jnp.dot(hb, wq_ref[...], preferred_element_type=F32) * (MEM_HEAD_DIM ** -0.5)
        qb = q.astype(BF16)
        scores.append([lax.dot_general(qb[:, sl], k_ref[0, :, sl], (((1,), (1,)), ((), ())),
                                       preferred_element_type=F32) for sl in heads])
    atts = []
    for sc in scores:
        outs = []
        for s, sl in zip(sc, heads):
            m = jnp.max(s, axis=-1, keepdims=True)
            p = jnp.exp(s - m)
            l = jnp.sum(p, axis=-1, keepdims=True)
            o = jnp.dot(p.astype(BF16), v_ref[0, :, sl], preferred_element_type=F32) / l
            outs.append(o.astype(BF16))
        atts.append(jnp.dot(jnp.concatenate(outs, axis=1), wo_ref[...], preferred_element_type=F32))
    for rows, att in zip(subs, atts):
        o_ref[0, rows, :] = _layernorm(ALPHA * h_ref[0, rows, :] + att, g_ref[...], b_ref[...])


def _mem_attn(h1, wq, km, vm, wo, g2, b2, *, ts):
    b, s, d = h1.shape
    nm = km.shape[1]
    tile = lambda bi, ji: (bi, ji, 0)
    const = lambda bi, ji: (0, 0)
    per_b = lambda bi, ji: (bi, 0, 0)
    return pl.pallas_call(
        functools.partial(_mem_attn_kernel, sub_rows=SUB_ROWS),
        grid=(b, s // ts),
        in_specs=[
            pl.BlockSpec((1, ts, d), tile),
            pl.BlockSpec((d, d), const, pipeline_mode=pl.Buffered(1)),
            pl.BlockSpec((1, nm, d), per_b),
            pl.BlockSpec((1, nm, d), per_b),
            pl.BlockSpec((d, d), const, pipeline_mode=pl.Buffered(1)),
            pl.BlockSpec((1, d), const),
            pl.BlockSpec((1, d), const),
        ],
        out_specs=pl.BlockSpec((1, ts, d), tile),
        out_shape=jax.ShapeDtypeStruct((b, s, d), F32),
        compiler_params=pltpu.CompilerParams(
            dimension_semantics=("arbitrary", "arbitrary"), vmem_limit_bytes=VMEM_LIMIT),
        name="mem_attn",
    )(h1, wq, km, vm, wo, g2, b2)


def _mlp_kernel(h_ref, w1_ref, b1_ref, w2_ref, b2_ref, g_ref, b_ref, o_ref, *, ff_chunk, sub_rows):
    for st in range(h_ref.shape[0] // sub_rows):
        rows = slice(st * sub_rows, (st + 1) * sub_rows)
        h = h_ref[rows, :]
        hb = h.astype(BF16)
        acc = ALPHA * h + b2_ref[...]
        for f in range(D_FF // ff_chunk):
            sl = slice(f * ff_chunk, (f + 1) * ff_chunk)
            a = jnp.dot(hb, w1_ref[:, sl], preferred_element_type=F32) + b1_ref[:, sl]
            a = jnp.maximum(a, 0.0)
            acc = acc + jnp.dot((a * a).astype(BF16), w2_ref[sl, :], preferred_element_type=F32)
        o_ref[rows, :] = _layernorm(acc, g_ref[...], b_ref[...])


def _mlp(h2, w1, b1, w2, b2, g3, b3, *, tm):
    m, d = h2.shape
    row = lambda i: (i, 0)
    const = lambda i: (0, 0)
    kern = functools.partial(_mlp_kernel, ff_chunk=1024, sub_rows=SUB_ROWS)
    return pl.pallas_call(
        kern,
        grid=(m // tm,),
        in_specs=[
            pl.BlockSpec((tm, d), row),
            pl.BlockSpec((d, D_FF), const, pipeline_mode=pl.Buffered(1)),
            pl.BlockSpec((1, D_FF), const),
            pl.BlockSpec((D_FF, d), const, pipeline_mode=pl.Buffered(1)),
            pl.BlockSpec((1, d), const),
            pl.BlockSpec((1, d), const),
            pl.BlockSpec((1, d), const),
        ],
        out_specs=pl.BlockSpec((tm, d), row),
        out_shape=jax.ShapeDtypeStruct((m, d), F32),
        compiler_params=pltpu.CompilerParams(
            dimension_semantics=("arbitrary",), vmem_limit_bytes=VMEM_LIMIT),
        name="mlp",
    )(h2, w1, b1, w2, b2, g3, b3)


def _rope_tables(seq_len, gain, scale):
    rows = seq_len // GRID_W
    row_ids = jnp.repeat(jnp.arange(rows, dtype=jnp.int32), GRID_W)
    col_ids = jnp.tile(jnp.arange(GRID_W, dtype=jnp.int32), rows)
    inv = ROPE_THETA ** (-jnp.arange(0, AXIS_DIM, 2, dtype=jnp.float32) / AXIS_DIM)
    ang = jnp.concatenate([row_ids[:, None].astype(jnp.float32) * inv,
                           col_ids[:, None].astype(jnp.float32) * inv], axis=-1)
    cos = jnp.repeat(jnp.cos(ang), 2, axis=-1)
    sin = jnp.repeat(jnp.sin(ang), 2, axis=-1)
    sign = jnp.where(jnp.arange(HEAD_DIM) % 2 == 0, -1.0, 1.0).astype(jnp.float32)
    gain_sw = gain.reshape(HEAD_DIM // 2, 2)[:, ::-1].reshape(HEAD_DIM)
    c = cos * gain * scale
    s = sin * sign * gain_sw * scale
    return jnp.tile(c, (1, LANES // HEAD_DIM)), jnp.tile(s, (1, LANES // HEAD_DIM))


def kernel(x, mem, ln_in_g, ln_in_b, w_in, q_norm_g, k_norm_g, conv_w, conv_b, conv_ln_g, conv_ln_b,
           attn_out_g, conv_out_g, w_out, ln1_g, ln1_b, w_mem_q, w_mem_kv, w_mem_o, ln2_g, ln2_b,
           w_ff1, b_ff1, w_ff2, b_ff2, ln3_g, ln3_b):
    b, s, d = x.shape
    assert d == D_MODEL and w_in.shape[0] == DEPTH == 1
    assert s % GRID_W == 0
    m = b * s
    nm = mem.shape[1]
    row = lambda a: a.reshape(1, -1)

    cq, sq = _rope_tables(s, q_norm_g[0], HEAD_DIM ** -0.5 * LOG2_E)
    ck, sk = _rope_tables(s, k_norm_g[0], 1.0)
    seg = jnp.arange(LANES) // HEAD_DIM
    bd = (seg[:, None] == seg[None, :]).astype(BF16)

    h, q, kd, vd, u = _in_proj(x.reshape(m, d), row(ln_in_g), row(ln_in_b), w_in[0].astype(BF16),
                               cq, sq, ck, sk, bd, seq=s, tm=IN_PROJ_ROWS)
    km, vm = _mem_kv(mem.reshape(b * nm, d), w_mem_kv[0], tm=MEM_KV_ROWS)

    score_bound = (HEAD_DIM * jnp.max(jnp.abs(q_norm_g[0])) * jnp.max(jnp.abs(k_norm_g[0]))
                   * (HEAD_DIM ** -0.5 * LOG2_E))
    fast = (score_bound <= SAFE_EXP2_BOUND).astype(jnp.int32).reshape(1)

    attn_n, conv, (w_out_b, w_mq_b, w_mo_b, w_ff1_b, w_ff2_b) = _attention(
        fast, q.reshape(b, s, ATTN_WIDTH), kd.reshape(b, s, 2 * LANES), vd.reshape(b, s, 4 * LANES),
        row(attn_out_g[0]), u.reshape(b, s, CONV_WIDTH), conv_w[0], row(conv_b[0]),
        [w_out[0], w_mem_q[0], w_mem_o[0], w_ff1[0], w_ff2[0]], tq=ATTN_Q_ROWS)

    h1 = _mix_out(conv, attn_n, h.reshape(b, s, d), row(conv_ln_g[0]), row(conv_ln_b[0]),
                  row(conv_out_g[0]), w_out_b, row(ln1_g[0]), row(ln1_b[0]), ts=MIX_ROWS)

    h2 = _mem_attn(h1, w_mq_b, km.reshape(b, nm, d), vm.reshape(b, nm, d), w_mo_b,
                   row(ln2_g[0]), row(ln2_b[0]), ts=MEM_ATTN_ROWS)

    out = _mlp(h2.reshape(m, d), w_ff1_b, row(b_ff1[0]), w_ff2_b, row(b_ff2[0]),
               row(ln3_g[0]), row(ln3_b[0]), tm=MLP_ROWS)
    return out.reshape(b, s, d)
```

```python
import functools

import jax
import jax.numpy as jnp
from jax import lax
from jax.experimental import pallas as pl
from jax.experimental.pallas import tpu as pltpu

D_MODEL = 1024
HEAD_DIM = 64
ATTN_HEADS = 8
KV_HEADS = 2
ATTN_WIDTH = ATTN_HEADS * HEAD_DIM
KV_WIDTH = KV_HEADS * HEAD_DIM
CONV_WIDTH = D_MODEL - ATTN_WIDTH
CONV_K = 31
CONV_HALO = 16
MEM_HEADS = 4
MEM_HEAD_DIM = D_MODEL // MEM_HEADS
D_FF = 4 * D_MODEL
GRID_W = 64
AXIS_DIM = HEAD_DIM // 2
ROPE_THETA = 10000.0
EPS = 1e-5
DEPTH = 1
ALPHA = (2 * DEPTH) ** 0.25
LOG2_E = 1.4426950408889634

LANES = 128
F32_SUBLANES = 8
VMEM_LIMIT = 56 * 1024 * 1024

IN_PROJ_ROWS = 1024
MEM_KV_ROWS = 1024
ATTN_Q_ROWS = 1024
MIX_ROWS = 1024
MEM_ATTN_ROWS = 1024
MLP_ROWS = 1024
SUB_ROWS = 256
CONV_CHUNK_ROWS = 32
CONV_LAG_CHAINS = 3
SAFE_EXP2_BOUND = 60.0

F32 = jnp.float32
BF16 = jnp.bfloat16


def _layernorm(z, g, b):
    mu = jnp.mean(z, axis=-1, keepdims=True)
    zc = z - mu
    var = jnp.mean(zc * zc, axis=-1, keepdims=True)
    return zc * lax.rsqrt(var + EPS) * g + b


def _seg_mean_sq(z, bd):
    s = z * z
    hi = s.astype(BF16)
    lo = (s - hi.astype(F32)).astype(BF16)
    tot = (jnp.dot(hi, bd, preferred_element_type=F32)
           + jnp.dot(lo, bd, preferred_element_type=F32))
    return tot * (1.0 / HEAD_DIM)


def _in_proj_kernel(x_ref, g_ref, b_ref, w_ref, cq_ref, sq_ref, ck_ref, sk_ref, bd_ref,
                    h_ref, q_ref, kd_ref, vd_ref, u_ref, *, sub_rows):
    bd = bd_ref[...]
    lane = lax.broadcasted_iota(jnp.int32, (sub_rows, LANES), 1)
    even = (lane & 1) == 0
    lo = lane < HEAD_DIM
    ones = jnp.ones((sub_rows, LANES), BF16)
    c0 = ATTN_WIDTH + 2 * KV_WIDTH

    def norm_rope(z, c, s):
        sw = jnp.where(even, pltpu.roll(z, LANES - 1, 1), pltpu.roll(z, 1, 1))
        r = lax.rsqrt(_seg_mean_sq(z, bd) + EPS)
        return r * (z * c + sw * s)

    subs = [slice(st * sub_rows, (st + 1) * sub_rows) for st in range(x_ref.shape[0] // sub_rows)]
    hbs = []
    for rows in subs:
        h = _layernorm(x_ref[rows, :], g_ref[...], b_ref[...])
        h_ref[rows, :] = h
        hbs.append(h.astype(BF16))
    projs = []
    for hb in hbs:
        projs.append((
            jnp.dot(hb, w_ref[:, 0:ATTN_WIDTH], preferred_element_type=F32),
            jnp.dot(hb, w_ref[:, ATTN_WIDTH:c0], preferred_element_type=F32),
            jnp.dot(hb, w_ref[:, c0:c0 + CONV_WIDTH], preferred_element_type=F32),
            jnp.dot(hb, w_ref[:, c0 + CONV_WIDTH:c0 + 2 * CONV_WIDTH], preferred_element_type=F32)))
    for rows, (zq, zkv, val, gate) in zip(subs, projs):
        u_ref[rows, :] = val * jax.nn.sigmoid(gate)
        cq = cq_ref[rows, :]
        sq = sq_ref[rows, :]
        for g in range(ATTN_WIDTH // LANES):
            sl = slice(g * LANES, (g + 1) * LANES)
            q_ref[rows, sl] = norm_rope(zq[:, sl], cq, sq).astype(BF16)
        kk = norm_rope(zkv[:, 0:LANES], ck_ref[rows, :], sk_ref[rows, :])
        kr = pltpu.roll(kk, HEAD_DIM, 1)
        kd_ref[rows, 0:LANES] = jnp.where(lo, kk, kr).astype(BF16)
        kd_ref[rows, LANES:2 * LANES] = jnp.where(lo, kr, kk).astype(BF16)
        vv = zkv[:, LANES:2 * LANES]
        vr = pltpu.roll(vv, HEAD_DIM, 1)
        vd_ref[rows, 0:LANES] = jnp.where(lo, vv, vr).astype(BF16)
        vd_ref[rows, LANES:2 * LANES] = ones
        vd_ref[rows, 2 * LANES:3 * LANES] = jnp.where(lo, vr, vv).astype(BF16)
        vd_ref[rows, 3 * LANES:4 * LANES] = ones


def _in_proj(x2, ln_g, ln_b, w_in, cq, sq, ck, sk, bd, *, seq, tm):
    m, d = x2.shape
    n_in = w_in.shape[1]
    ns = seq // tm
    row = lambda i: (i, 0)
    const = lambda i: (0, 0)
    tab = lambda i: (i % ns, 0)
    return pl.pallas_call(
        functools.partial(_in_proj_kernel, sub_rows=SUB_ROWS),
        grid=(m // tm,),
        in_specs=[
            pl.BlockSpec((tm, d), row),
            pl.BlockSpec((1, d), const),
            pl.BlockSpec((1, d), const),
            pl.BlockSpec((d, n_in), const, pipeline_mode=pl.Buffered(1)),
            pl.BlockSpec((tm, LANES), tab),
            pl.BlockSpec((tm, LANES), tab),
            pl.BlockSpec((tm, LANES), tab),
            pl.BlockSpec((tm, LANES), tab),
            pl.BlockSpec((LANES, LANES), const),
        ],
        out_specs=[
            pl.BlockSpec((tm, d), row),
            pl.BlockSpec((tm, ATTN_WIDTH), row),
            pl.BlockSpec((tm, 2 * LANES), row),
            pl.BlockSpec((tm, 4 * LANES), row),
            pl.BlockSpec((tm, CONV_WIDTH), row),
        ],
        out_shape=[
            jax.ShapeDtypeStruct((m, d), F32),
            jax.ShapeDtypeStruct((m, ATTN_WIDTH), BF16),
            jax.ShapeDtypeStruct((m, 2 * LANES), BF16),
            jax.ShapeDtypeStruct((m, 4 * LANES), BF16),
            jax.ShapeDtypeStruct((m, CONV_WIDTH), F32),
        ],
        compiler_params=pltpu.CompilerParams(
            dimension_semantics=("arbitrary",), vmem_limit_bytes=VMEM_LIMIT),
        name="in_proj",
    )(x2, ln_g, ln_b, w_in, cq, sq, ck, sk, bd)


def _mem_kv_kernel(m_ref, w_ref, k_ref, v_ref):
    mb = m_ref[...].astype(BF16)
    wk = w_ref[:, 0:D_MODEL].astype(BF16)
    wv = w_ref[:, D_MODEL:2 * D_MODEL].astype(BF16)
    k_ref[...] = jnp.dot(mb, wk, preferred_element_type=F32).astype(BF16)
    v_ref[...] = jnp.dot(mb, wv, preferred_element_type=F32).astype(BF16)


def _mem_kv(mem2, w_kv, *, tm):
    m, d = mem2.shape
    row = lambda i: (i, 0)
    return pl.pallas_call(
        _mem_kv_kernel,
        grid=(m // tm,),
        in_specs=[pl.BlockSpec((tm, d), row),
                  pl.BlockSpec((d, 2 * d), lambda i: (0, 0), pipeline_mode=pl.Buffered(1))],
        out_specs=[pl.BlockSpec((tm, d), row), pl.BlockSpec((tm, d), row)],
        out_shape=[jax.ShapeDtypeStruct((m, d), BF16), jax.ShapeDtypeStruct((m, d), BF16)],
        compiler_params=pltpu.CompilerParams(
            dimension_semantics=("arbitrary",), vmem_limit_bytes=VMEM_LIMIT),
        name="mem_kv",
    )(mem2, w_kv)


def _conv_chunk(upad_ref, wb_ref, bias, row0, rc, prev, never):
    sub = F32_SUBLANES
    wrows = rc + 2 * CONV_HALO
    off0 = CONV_HALO - CONV_K // 2
    win = upad_ref[pl.ds(row0, wrows), :]
    shifted = [win] + [pltpu.roll(win, wrows - r, 0) for r in range(1, sub)]
    tiles = []
    for i in range(rc // sub):
        acc = bias if prev is None else jnp.where(never, prev, bias)
        for r in range(sub):
            for a in range((2 * CONV_HALO) // sub):
                t = sub * a + r - off0
                if 0 <= t < CONV_K:
                    lo_row = sub * (a + i)
                    acc = acc + shifted[r][lo_row:lo_row + sub, :] * wb_ref[t]
        tiles.append(acc)
        prev = acc
    return tiles


def _attention_kernel(q_ref, k_ref, v_ref, g_ref, u_ref, cw_ref, cb_ref, *rest,
                      sub_rows, rows_per_chunk, n_cast, conv_lag, shift_by_max):
    w_refs = rest[:n_cast]
    o_ref, c_ref = rest[n_cast:n_cast + 2]
    wb16_refs = rest[n_cast + 2:2 * n_cast + 2]
    upad_ref, wb_ref = rest[2 * n_cast + 2:]
    for w_ref, wb16_ref in zip(w_refs, wb16_refs):
        wb16_ref[...] = w_ref[...].astype(wb16_ref.dtype)
    tq = q_ref.shape[1]
    s = u_ref.shape[1]
    cw = u_ref.shape[2]
    j = pl.program_id(2)

    @pl.when(j == 0)
    def _():
        zeros = jnp.zeros((CONV_HALO, cw), F32)
        upad_ref[0:CONV_HALO, :] = zeros
        upad_ref[CONV_HALO + s:2 * CONV_HALO + s, :] = zeros
        upad_ref[CONV_HALO:CONV_HALO + s, :] = u_ref[0]
        for t in range(CONV_K):
            wb_ref[t] = jnp.broadcast_to(cw_ref[t:t + 1, :], (F32_SUBLANES, cw))

    q = q_ref[0]
    k = k_ref[0]
    v = v_ref[0]
    lane = lax.broadcasted_iota(jnp.int32, (sub_rows, LANES), 1)
    lo = lane < HEAD_DIM
    never = lane >= LANES + j
    bias = jnp.broadcast_to(cb_ref[...], (F32_SUBLANES, cw))
    rc = rows_per_chunk
    chains = [(rb, g, keep_lo) for rb in range(tq // sub_rows) for g in range(2)
              for keep_lo in (True, False)]
    n_chunks = tq // rc

    never_tile = lax.broadcasted_iota(jnp.int32, (F32_SUBLANES, cw), 1) >= cw + j
    last_tile = [None]

    def conv_chunk(ci):
        r0 = ci * rc
        tiles = _conv_chunk(upad_ref, wb_ref, bias, pl.multiple_of(j * tq + r0, rc), rc,
                            last_tile[0], never_tile)
        last_tile[0] = tiles[-1]
        c_ref[0, r0:r0 + rc, :] = jnp.concatenate(tiles, axis=0)
        return tiles

    def zero_after(tiles):
        z = jnp.zeros((sub_rows, LANES), F32)
        if tiles:
            t = functools.reduce(lambda x, y: x + y, tiles)
            t = functools.reduce(lambda x, y: x + y,
                                 [t[:, c * LANES:(c + 1) * LANES] for c in range(cw // LANES)])
            z = jnp.where(never, jnp.broadcast_to(t[0:1, :], z.shape), z)
        return z.astype(q.dtype)

    def one_head(qg, keep_lo, zero, shift_by_max):
        lhs = jnp.where(lo, qg, zero) if keep_lo else jnp.where(lo, zero, qg)
        sc = lax.dot_general(lhs, k, (((1,), (1,)), ((), ())), preferred_element_type=F32)
        if shift_by_max:
            sc = sc - jnp.max(sc, axis=-1, keepdims=True)
        p = jnp.exp2(sc)
        if shift_by_max:
            p = p.astype(BF16)
        r = lax.dot_general(p, v, (((1,), (0,)), ((), ())),
                            preferred_element_type=F32)
        o = r[:, 0:LANES] / r[:, LANES:2 * LANES]
        return o * lax.rsqrt(jnp.mean(o * o, axis=-1, keepdims=True) + EPS)

    def attend(shift_by_max, groups):
        fed = []
        done = {}
        for ci, (rb, g, keep_lo) in enumerate(chains):
            fed.append([t for c in groups[ci] for t in conv_chunk(c)])
            rows = slice(rb * sub_rows, (rb + 1) * sub_rows)
            sl = slice(g * LANES, (g + 1) * LANES)
            zero = zero_after(fed[ci - conv_lag] if ci >= conv_lag else [])
            done[keep_lo] = one_head(q[rows, sl], keep_lo, zero, shift_by_max)
            if not keep_lo:
                og = jnp.where(lo, done[True], done[False])
                o_ref[0, rows, sl] = (og * g_ref[:, sl]).astype(o_ref.dtype)

    if shift_by_max:
        per_block = n_chunks // (tq // sub_rows)
        groups = [list(range(i // 4 * per_block, (i // 4 + 1) * per_block)) if i % 4 == 3 else []
                  for i in range(len(chains))]
    else:
        n_fed = len(chains) - conv_lag
        groups = [list(range((n_chunks * i) // n_fed, (n_chunks * (i + 1)) // n_fed))
                  if i < n_fed else [] for i in range(len(chains))]
    attend(shift_by_max, groups)


def _attention(q, kd, vd, gain, u, conv_w, conv_b, weights, *, tq, shift_by_max):
    b, s, _ = q.shape
    nj = s // tq
    n_steps = b * KV_HEADS * nj
    cw = CONV_WIDTH // KV_HEADS
    tile = lambda bi, hi, ji: (bi, ji, hi)
    per_bh = lambda bi, hi, ji: (bi, 0, hi)
    per_h = lambda bi, hi, ji: (0, hi)
    slab = lambda bi, hi, ji: ((bi * KV_HEADS + hi) * nj + ji, 0)
    w_specs = [pl.BlockSpec((w.shape[0] // n_steps, w.shape[1]), slab) for w in weights]
    kern = functools.partial(_attention_kernel, sub_rows=SUB_ROWS, rows_per_chunk=CONV_CHUNK_ROWS,
                             n_cast=len(weights), conv_lag=CONV_LAG_CHAINS,
                             shift_by_max=shift_by_max)
    outs = pl.pallas_call(
        kern,
        grid=(b, KV_HEADS, nj),
        in_specs=[
            pl.BlockSpec((1, tq, 2 * LANES), tile),
            pl.BlockSpec((1, s, LANES), per_bh),
            pl.BlockSpec((1, s, 2 * LANES), per_bh),
            pl.BlockSpec((1, 2 * LANES), per_h),
            pl.BlockSpec((1, s, cw), per_bh),
            pl.BlockSpec((CONV_K, cw), per_h),
            pl.BlockSpec((1, cw), per_h),
        ] + w_specs,
        out_specs=[
            pl.BlockSpec((1, tq, 2 * LANES), tile),
            pl.BlockSpec((1, tq, cw), tile),
        ] + w_specs,
        out_shape=[
            jax.ShapeDtypeStruct((b, s, ATTN_WIDTH), BF16),
            jax.ShapeDtypeStruct((b, s, CONV_WIDTH), F32),
        ] + [jax.ShapeDtypeStruct(w.shape, BF16) for w in weights],
        scratch_shapes=[
            pltpu.VMEM((s + 2 * CONV_HALO, cw), F32),
            pltpu.VMEM((CONV_K, F32_SUBLANES, cw), F32),
        ],
        compiler_params=pltpu.CompilerParams(
            dimension_semantics=("arbitrary", "arbitrary", "arbitrary"),
            vmem_limit_bytes=VMEM_LIMIT),
        name="attention",
    )(q, kd, vd, gain, u, conv_w, conv_b, *weights)
    return outs[0], outs[1], tuple(outs[2:])


def _seg_mean_sq_lanes(z):
    lo = lax.broadcasted_iota(jnp.int32, z.shape, 1) < HEAD_DIM
    s = z * z
    s_lo = jnp.sum(jnp.where(lo, s, 0.0), axis=-1, keepdims=True)
    s_hi = jnp.sum(jnp.where(lo, 0.0, s), axis=-1, keepdims=True)
    return jnp.where(lo, s_lo, s_hi) * (1.0 / HEAD_DIM)


def _mix_out_kernel(c_ref, a_ref, h_ref, lg_ref, lb_ref, cog_ref,
                    wo_ref, g1_ref, b1_ref, o_ref, *, sub_rows):
    subs =[slice(st * sub_rows, (st + 1) * sub_rows) for st in range(a_ref.shape[1] // sub_rows)]
    ycs = []
    for rows in subs:
        c = _layernorm(c_ref[0, rows, :], lg_ref[...], lb_ref[...])
        c = c * jax.nn.sigmoid(c)
        parts = []
        for g in range(CONV_WIDTH // LANES):
            sl = slice(g * LANES, (g + 1) * LANES)
            cg = c[:, sl]
            parts.append((cg * lax.rsqrt(_seg_mean_sq_lanes(cg) + EPS) * cog_ref[:, sl]).astype(BF16))
        ycs.append(jnp.concatenate(parts, axis=1))
    mixes = []
    for rows, yc in zip(subs, ycs):
        mixes.append(jnp.dot(a_ref[0, rows, :], wo_ref[0:ATTN_WIDTH, :], preferred_element_type=F32)
                     + jnp.dot(yc, wo_ref[ATTN_WIDTH:D_MODEL, :], preferred_element_type=F32))
    for rows, mix in zip(subs, mixes):
        o_ref[0, rows, :] = _layernorm(ALPHA * h_ref[0, rows, :] + mix, g1_ref[...], b1_ref[...])


def _mix_out(conv, attn_n, h, ln_g, ln_b, out_g, w_out, g1, b1, *, ts):
    b, s, d = h.shape
    tile = lambda bi, ji: (bi, ji, 0)
    const = lambda bi, ji: (0, 0)
    return pl.pallas_call(
        functools.partial(_mix_out_kernel, sub_rows=SUB_ROWS),
        grid=(b, s // ts),
        in_specs=[
            pl.BlockSpec((1, ts, CONV_WIDTH), tile),
            pl.BlockSpec((1, ts, ATTN_WIDTH), tile),
            pl.BlockSpec((1, ts, d), tile),
            pl.BlockSpec((1, CONV_WIDTH), const),
            pl.BlockSpec((1, CONV_WIDTH), const),
            pl.BlockSpec((1, CONV_WIDTH), const),
            pl.BlockSpec((d, d), const, pipeline_mode=pl.Buffered(1)),
            pl.BlockSpec((1, d), const),
            pl.BlockSpec((1, d), const),
        ],
        out_specs=pl.BlockSpec((1, ts, d), tile),
        out_shape=jax.ShapeDtypeStruct((b, s, d), F32),
        compiler_params=pltpu.CompilerParams(
            dimension_semantics=("arbitrary", "arbitrary"), vmem_limit_bytes=VMEM_LIMIT),
        name="mix_out",
    )(conv, attn_n, h, ln_g, ln_b, out_g, w_out, g1, b1)


def _mem_attn_kernel(h_ref, wq_ref, k_ref, v_ref, wo_ref, g_ref, b_ref, o_ref, *, sub_rows):
    subs = [slice(st * sub_rows, (st + 1) * sub_rows) for st in range(h_ref.shape[1] // sub_rows)]
    heads = [slice(hd * MEM_HEAD_DIM, (hd + 1) * MEM_HEAD_DIM) for hd in range(MEM_HEADS)]
    scores = []
    for rows in subs:
        hb = h_ref[0, rows, :].astype(BF16)
        q = jnp.dot(hb, wq_ref[...], preferred_element_type=F32) * (MEM_HEAD_DIM ** -0.5)
        qb = q.astype(BF16)
        scores.append([lax.dot_general(qb[:, sl], k_ref[0, :, sl], (((1,), (1,)), ((), ())),
                                       preferred_element_type=F32) for sl in heads])
    atts = []
    for sc in scores:
        outs = []
        for s, sl in zip(sc, heads):
            m = jnp.max(s, axis=-1, keepdims=True)
            p = jnp.exp(s - m)
            l = jnp.sum(p, axis=-1, keepdims=True)
            o = jnp.dot(p.astype(BF16), v_ref[0, :, sl], preferred_element_type=F32) / l
            outs.append(o.astype(BF16))
        atts.append(jnp.dot(jnp.concatenate(outs, axis=1), wo_ref[...], preferred_element_type=F32))
    for rows, att in zip(subs, atts):
        o_ref[0, rows, :] = _layernorm(ALPHA * h_ref[0, rows, :] + att, g_ref[...], b_ref[...])


def _mem_attn(h1, wq, km, vm, wo, g2, b2, *, ts):
    b, s, d = h1.shape
    nm = km.shape[1]
    tile = lambda bi, ji: (bi, ji, 0)
    const = lambda bi, ji: (0, 0)
    per_b = lambda bi, ji: (bi, 0, 0)
    return pl.pallas_call(
        functools.partial(_mem_attn_kernel, sub_rows=SUB_ROWS),
        grid=(b, s // ts),
        in_specs=[
            pl.BlockSpec((1, ts, d), tile),
            pl.BlockSpec((d, d), const, pipeline_mode=pl.Buffered(1)),
            pl.BlockSpec((1, nm, d), per_b),
            pl.BlockSpec((1, nm, d), per_b),
            pl.BlockSpec((d, d), const, pipeline_mode=pl.Buffered(1)),
            pl.BlockSpec((1, d), const),
            pl.BlockSpec((1, d), const),
        ],
        out_specs=pl.BlockSpec((1, ts, d), tile),
        out_shape=jax.ShapeDtypeStruct((b, s, d), F32),
        compiler_params=pltpu.CompilerParams(
            dimension_semantics=("arbitrary", "arbitrary"), vmem_limit_bytes=VMEM_LIMIT),
        name="mem_attn",
    )(h1, wq, km, vm, wo, g2, b2)


def _mlp_kernel(h_ref, w1_ref, b1_ref, w2_ref, b2_ref, g_ref, b_ref, o_ref, *, ff_chunk, sub_rows):
    for st in range(h_ref.shape[0] // sub_rows):
        rows = slice(st * sub_rows, (st + 1) * sub_rows)
        h = h_ref[rows, :]
        hb = h.astype(BF16)
        acc = ALPHA * h + b2_ref[...]
        for f in range(D_FF // ff_chunk):
            sl = slice(f * ff_chunk, (f + 1) * ff_chunk)
            a = jnp.dot(hb, w1_ref[:, sl], preferred_element_type=F32) + b1_ref[:, sl]
            a = jnp.maximum(a, 0.0)
            acc = acc + jnp.dot((a * a).astype(BF16), w2_ref[sl, :], preferred_element_type=F32)
        o_ref[rows, :] = _layernorm(acc, g_ref[...], b_ref[...])


def _mlp(h2, w1, b1, w2, b2, g3, b3, *, tm):
    m, d = h2.shape
    row = lambda i: (i, 0)
    const = lambda i: (0, 0)
    kern = functools.partial(_mlp_kernel, ff_chunk=1024, sub_rows=SUB_ROWS)
    return pl.pallas_call(
        kern,
        grid=(m // tm,),
        in_specs=[
            pl.BlockSpec((tm, d), row),
            pl.BlockSpec((d, D_FF), const, pipeline_mode=pl.Buffered(1)),
            pl.BlockSpec((1, D_FF), const),
            pl.BlockSpec((D_FF, d), const, pipeline_mode=pl.Buffered(1)),
            pl.BlockSpec((1, d), const),
            pl.BlockSpec((1, d), const),
            pl.BlockSpec((1, d), const),
        ],
        out_specs=pl.BlockSpec((tm, d), row),
        out_shape=jax.ShapeDtypeStruct((m, d), F32),
        compiler_params=pltpu.CompilerParams(
            dimension_semantics=("arbitrary",), vmem_limit_bytes=VMEM_LIMIT),
        name="mlp",
    )(h2, w1, b1, w2, b2, g3, b3)


def _rope_tables(seq_len, gain, scale):
    rows = seq_len // GRID_W
    row_ids = jnp.repeat(jnp.arange(rows, dtype=jnp.int32), GRID_W)
    col_ids = jnp.tile(jnp.arange(GRID_W, dtype=jnp.int32), rows)
    inv = ROPE_THETA ** (-jnp.arange(0, AXIS_DIM, 2, dtype=jnp.float32) / AXIS_DIM)
    ang = jnp.concatenate([row_ids[:, None].astype(jnp.float32) * inv,
                           col_ids[:, None].astype(jnp.float32) * inv], axis=-1)
    cos = jnp.repeat(jnp.cos(ang), 2, axis=-1)
    sin = jnp.repeat(jnp.sin(ang), 2, axis=-1)
    sign = jnp.where(jnp.arange(HEAD_DIM) % 2 == 0, -1.0, 1.0).astype(jnp.float32)
    gain_sw = gain.reshape(HEAD_DIM // 2, 2)[:, ::-1].reshape(HEAD_DIM)
    c = cos * gain * scale
    s = sin * sign * gain_sw * scale
    return jnp.tile(c, (1, LANES // HEAD_DIM)), jnp.tile(s, (1, LANES // HEAD_DIM))


def kernel(x, mem, ln_in_g, ln_in_b, w_in, q_norm_g, k_norm_g, conv_w, conv_b, conv_ln_g, conv_ln_b,
           attn_out_g, conv_out_g, w_out, ln1_g, ln1_b, w_mem_q, w_mem_kv, w_mem_o, ln2_g, ln2_b,
           w_ff1, b_ff1, w_ff2, b_ff2, ln3_g, ln3_b):
    b, s, d = x.shape
    assert d == D_MODEL and w_in.shape[0] == DEPTH == 1
    assert s % GRID_W == 0
    m = b * s
    nm = mem.shape[1]
    row = lambda a: a.reshape(1, -1)

    cq, sq = _rope_tables(s, q_norm_g[0], HEAD_DIM ** -0.5 * LOG2_E)
    ck, sk = _rope_tables(s, k_norm_g[0], 1.0)
    seg = jnp.arange(LANES) // HEAD_DIM
    bd = (seg[:, None] == seg[None, :]).astype(BF16)

    h, q, kd, vd, u = _in_proj(x.reshape(m, d), row(ln_in_g), row(ln_in_b), w_in[0].astype(BF16),
                               cq, sq, ck, sk, bd, seq=s, tm=IN_PROJ_ROWS)
    km, vm = _mem_kv(mem.reshape(b * nm, d), w_mem_kv[0], tm=MEM_KV_ROWS)

    score_bound = (HEAD_DIM * jnp.max(jnp.abs(q_norm_g[0])) * jnp.max(jnp.abs(k_norm_g[0]))
                   * (HEAD_DIM ** -0.5 * LOG2_E))
    attn_n, conv, (w_out_b, w_mq_b, w_mo_b, w_ff1_b, w_ff2_b) = lax.cond(
        score_bound <= SAFE_EXP2_BOUND,
        functools.partial(_attention, tq=ATTN_Q_ROWS, shift_by_max=False),
        functools.partial(_attention, tq=ATTN_Q_ROWS, shift_by_max=True),
        q.reshape(b, s, ATTN_WIDTH), kd.reshape(b, s, 2 * LANES), vd.reshape(b, s, 4 * LANES),
        row(attn_out_g[0]), u.reshape(b, s, CONV_WIDTH), conv_w[0], row(conv_b[0]),
        (w_out[0], w_mem_q[0], w_mem_o[0], w_ff1[0], w_ff2[0]))

    h1 = _mix_out(conv, attn_n, h.reshape(b, s, d), row(conv_ln_g[0]), row(conv_ln_b[0]),
                  row(conv_out_g[0]), w_out_b, row(ln1_g[0]), row(ln1_b[0]), ts=MIX_ROWS)

    h2 = _mem_attn(h1, w_mq_b, km.reshape(b, nm, d), vm.reshape(b, nm, d), w_mo_b,
                   row(ln2_g[0]), row(ln2_b[0]), ts=MEM_ATTN_ROWS)

    out = _mlp(h2.reshape(m, d), w_ff1_b, row(b_ff1[0]), w_ff2_b, row(b_ff2[0]),
               row(ln3_g[0]), row(ln3_b[0]), tm=MLP_ROWS)
    return out.reshape(b, s, d)
```

```python
import functools

import jax
import jax.numpy as jnp
from jax import lax
from jax.experimental import pallas as pl
from jax.experimental.pallas import tpu as pltpu

D_MODEL = 1024
HEAD_DIM = 64
ATTN_HEADS = 8
KV_HEADS = 2
ATTN_WIDTH = ATTN_HEADS * HEAD_DIM
KV_WIDTH = KV_HEADS * HEAD_DIM
CONV_WIDTH = D_MODEL - ATTN_WIDTH
CONV_K = 31
CONV_HALO = 16
MEM_HEADS = 4
MEM_HEAD_DIM = D_MODEL // MEM_HEADS
D_FF = 4 * D_MODEL
GRID_W = 64
AXIS_DIM = HEAD_DIM // 2
ROPE_THETA = 10000.0
EPS = 1e-5
DEPTH = 1
ALPHA = (2 * DEPTH) ** 0.25
LOG2_E = 1.4426950408889634

LANES = 128
F32_SUBLANES = 8
VMEM_LIMIT = 56 * 1024 * 1024

IN_PROJ_ROWS = 1024
MEM_KV_ROWS = 1024
ATTN_Q_ROWS = 1024
MIX_ROWS = 1024
MEM_ATTN_ROWS = 1024
MLP_ROWS = 1024
SUB_ROWS = 256
CONV_CHUNK_ROWS = 32
CONV_LAG_CHAINS = 3
SAFE_EXP2_BOUND = 60.0
MLP_FF_CHUNK = 1024

F32 = jnp.float32
BF16 = jnp.bfloat16


def _layernorm(z, g, b):
    mu = jnp.mean(z, axis=-1, keepdims=True)
    zc = z - mu
    var = jnp.mean(zc * zc, axis=-1, keepdims=True)
    return zc * lax.rsqrt(var + EPS) * g + b


def _seg_mean_sq(z, bd):
    s = z * z
    hi = s.astype(BF16)
    lo = (s - hi.astype(F32)).astype(BF16)
    tot = (jnp.dot(hi, bd, preferred_element_type=F32)
           + jnp.dot(lo, bd, preferred_element_type=F32))
    return tot * (1.0 / HEAD_DIM)


def _in_proj_kernel(x_ref, g_ref, b_ref, w_ref, cq_ref, sq_ref, ck_ref, sk_ref, bd_ref,
                    h_ref, q_ref, kd_ref, vd_ref, u_ref, *, sub_rows):
    bd = bd_ref[...]
    lane = lax.broadcasted_iota(jnp.int32, (sub_rows, LANES), 1)
    even = (lane & 1) == 0
    lo = lane < HEAD_DIM
    ones = jnp.ones((sub_rows, LANES), BF16)
    c0 = ATTN_WIDTH + 2 * KV_WIDTH

    def norm_rope(z, c, s):
        sw = jnp.where(even, pltpu.roll(z, LANES - 1, 1), pltpu.roll(z, 1, 1))
        r = lax.rsqrt(_seg_mean_sq(z, bd) + EPS)
        return r * (z * c + sw * s)

    subs = [slice(st * sub_rows, (st + 1) * sub_rows) for st in range(x_ref.shape[0] // sub_rows)]
    hbs = []
    for rows in subs:
        h = _layernorm(x_ref[rows, :], g_ref[...], b_ref[...])
        h_ref[rows, :] = h
        hbs.append(h.astype(BF16))
    projs = []
    for hb in hbs:
        projs.append((
            jnp.dot(hb, w_ref[:, 0:ATTN_WIDTH], preferred_element_type=F32),
            jnp.dot(hb, w_ref[:, ATTN_WIDTH:c0], preferred_element_type=F32),
            jnp.dot(hb, w_ref[:, c0:c0 + CONV_WIDTH], preferred_element_type=F32),
            jnp.dot(hb, w_ref[:, c0 + CONV_WIDTH:c0 + 2 * CONV_WIDTH], preferred_element_type=F32)))
    for rows, (zq, zkv, val, gate) in zip(subs, projs):
        u_ref[rows, :] = val * jax.nn.sigmoid(gate)
        cq = cq_ref[rows, :]
        sq = sq_ref[rows, :]
        for g in range(ATTN_WIDTH // LANES):
            sl = slice(g * LANES, (g + 1) * LANES)
            q_ref[rows, sl] = norm_rope(zq[:, sl], cq, sq).astype(BF16)
        kk = norm_rope(zkv[:, 0:LANES], ck_ref[rows, :], sk_ref[rows, :])
        kr = pltpu.roll(kk, HEAD_DIM, 1)
        kd_ref[rows, 0:LANES] = jnp.where(lo, kk, kr).astype(BF16)
        kd_ref[rows, LANES:2 * LANES] = jnp.where(lo, kr, kk).astype(BF16)
        vv = zkv[:, LANES:2 * LANES]
        vr = pltpu.roll(vv, HEAD_DIM, 1)
        vd_ref[rows, 0:LANES] = jnp.where(lo, vv, vr).astype(BF16)
        vd_ref[rows, LANES:2 * LANES] = ones
        vd_ref[rows, 2 * LANES:3 * LANES] = jnp.where(lo, vr, vv).astype(BF16)
        vd_ref[rows, 3 * LANES:4 * LANES] = ones


def _in_proj(x2, ln_g, ln_b, w_in, cq, sq, ck, sk, bd, *, seq, tm):
    m, d = x2.shape
    n_in = w_in.shape[1]
    ns = seq // tm
    row = lambda i: (i, 0)
    const = lambda i: (0, 0)
    tab = lambda i: (i % ns, 0)
    return pl.pallas_call(
        functools.partial(_in_proj_kernel, sub_rows=SUB_ROWS),
        grid=(m // tm,),
        in_specs=[
            pl.BlockSpec((tm, d), row),
            pl.BlockSpec((1, d), const),
            pl.BlockSpec((1, d), const),
            pl.BlockSpec((d, n_in), const, pipeline_mode=pl.Buffered(1)),
            pl.BlockSpec((tm, LANES), tab),
            pl.BlockSpec((tm, LANES), tab),
            pl.BlockSpec((tm, LANES), tab),
            pl.BlockSpec((tm, LANES), tab),
            pl.BlockSpec((LANES, LANES), const),
        ],
        out_specs=[
            pl.BlockSpec((tm, d), row),
            pl.BlockSpec((tm, ATTN_WIDTH), row),
            pl.BlockSpec((tm, 2 * LANES), row),
            pl.BlockSpec((tm, 4 * LANES), row),
            pl.BlockSpec((tm, CONV_WIDTH), row),
        ],
        out_shape=[
            jax.ShapeDtypeStruct((m, d), F32),
            jax.ShapeDtypeStruct((m, ATTN_WIDTH), BF16),
            jax.ShapeDtypeStruct((m, 2 * LANES), BF16),
            jax.ShapeDtypeStruct((m, 4 * LANES), BF16),
            jax.ShapeDtypeStruct((m, CONV_WIDTH), F32),
        ],
        compiler_params=pltpu.CompilerParams(
            dimension_semantics=("arbitrary",), vmem_limit_bytes=VMEM_LIMIT),
        name="in_proj",
    )(x2, ln_g, ln_b, w_in, cq, sq, ck, sk, bd)


def _mem_kv_kernel(m_ref, w_ref, k_ref, v_ref):
    mb = m_ref[...].astype(BF16)
    wk = w_ref[:, 0:D_MODEL].astype(BF16)
    wv = w_ref[:, D_MODEL:2 * D_MODEL].astype(BF16)
    k_ref[...] = jnp.dot(mb, wk, preferred_element_type=F32).astype(BF16)
    v_ref[...] = jnp.dot(mb, wv, preferred_element_type=F32).astype(BF16)


def _mem_kv(mem2, w_kv, *, tm):
    m, d = mem2.shape
    row = lambda i: (i, 0)
    return pl.pallas_call(
        _mem_kv_kernel,
        grid=(m // tm,),
        in_specs=[pl.BlockSpec((tm, d), row),
                  pl.BlockSpec((d, 2 * d), lambda i: (0, 0), pipeline_mode=pl.Buffered(1))],
        out_specs=[pl.BlockSpec((tm, d), row), pl.BlockSpec((tm, d), row)],
        out_shape=[jax.ShapeDtypeStruct((m, d), BF16), jax.ShapeDtypeStruct((m, d), BF16)],
        compiler_params=pltpu.CompilerParams(
            dimension_semantics=("arbitrary",), vmem_limit_bytes=VMEM_LIMIT),
        name="mem_kv",
    )(mem2, w_kv)


def _conv_chunk(upad_ref, wb_ref, bias, row0, rc, prev, never):
    sub = F32_SUBLANES
    wrows = rc + 2 * CONV_HALO
    off0 = CONV_HALO - CONV_K // 2
    win = upad_ref[pl.ds(row0, wrows), :]
    shifted = [win] + [pltpu.roll(win, wrows - r, 0) for r in range(1, sub)]
    tiles = []
    for i in range(rc // sub):
        acc = bias if prev is None else jnp.where(never, prev, bias)
        for r in range(sub):
            for a in range((2 * CONV_HALO) // sub):
                t = sub * a + r - off0
                if 0 <= t < CONV_K:
                    lo_row = sub * (a + i)
                    acc = acc + shifted[r][lo_row:lo_row + sub, :] * wb_ref[t]
        tiles.append(acc)
        prev = acc
    return tiles


def _attention_kernel(q_ref, k_ref, v_ref, g_ref, u_ref, cw_ref, cb_ref, *rest,
                      sub_rows, rows_per_chunk, n_cast, conv_lag, shift_by_max):
    w_refs = rest[:n_cast]
    o_ref, c_ref = rest[n_cast:n_cast + 2]
    wb16_refs = rest[n_cast + 2:2 * n_cast + 2]
    upad_ref, wb_ref = rest[2 * n_cast + 2:]
    for w_ref, wb16_ref in zip(w_refs, wb16_refs):
        wb16_ref[...] = w_ref[...].astype(wb16_ref.dtype)
    tq = q_ref.shape[1]
    s = u_ref.shape[1]
    cw = u_ref.shape[2]
    j = pl.program_id(2)

    @pl.when(j == 0)
    def _():
        zeros = jnp.zeros((CONV_HALO, cw), F32)
        upad_ref[0:CONV_HALO, :] = zeros
        upad_ref[CONV_HALO + s:2 * CONV_HALO + s, :] = zeros
        upad_ref[CONV_HALO:CONV_HALO + s, :] = u_ref[0]
        for t in range(CONV_K):
            wb_ref[t] = jnp.broadcast_to(cw_ref[t:t + 1, :], (F32_SUBLANES, cw))

    q = q_ref[0]
    k = k_ref[0]
    v = v_ref[0]
    lane = lax.broadcasted_iota(jnp.int32, (sub_rows, LANES), 1)
    lo = lane < HEAD_DIM
    never = lane >= LANES + j
    bias = jnp.broadcast_to(cb_ref[...], (F32_SUBLANES, cw))
    rc = rows_per_chunk
    chains = [(rb, g, keep_lo) for rb in range(tq // sub_rows) for g in range(2)
              for keep_lo in (True, False)]
    n_chunks = tq // rc

    never_tile = lax.broadcasted_iota(jnp.int32, (F32_SUBLANES, cw), 1) >= cw + j
    last_tile = [None]

    def conv_chunk(ci):
        r0 = ci * rc
        tiles = _conv_chunk(upad_ref, wb_ref, bias, pl.multiple_of(j * tq + r0, rc), rc,
                            last_tile[0], never_tile)
        last_tile[0] = tiles[-1]
        c_ref[0, r0:r0 + rc, :] = jnp.concatenate(tiles, axis=0)
        return tiles

    def zero_after(tiles):
        z = jnp.zeros((sub_rows, LANES), F32)
        if tiles:
            t = functools.reduce(lambda x, y: x + y, tiles)
            t = functools.reduce(lambda x, y: x + y,
                                 [t[:, c * LANES:(c + 1) * LANES] for c in range(cw // LANES)])
            z = jnp.where(never, jnp.broadcast_to(t[0:1, :], z.shape), z)
        return z.astype(q.dtype)

    def one_head(qg, keep_lo, zero, shift_by_max):
        lhs = jnp.where(lo, qg, zero) if keep_lo else jnp.where(lo, zero, qg)
        sc = lax.dot_general(lhs, k, (((1,), (1,)), ((), ())), preferred_element_type=F32)
        if shift_by_max:
            sc = sc - jnp.max(sc, axis=-1, keepdims=True)
        p = jnp.exp2(sc)
        if shift_by_max:
            p = p.astype(BF16)
        r = lax.dot_general(p, v, (((1,), (0,)), ((), ())),
                            preferred_element_type=F32)
        o = r[:, 0:LANES] / r[:, LANES:2 * LANES]
        return o * lax.rsqrt(jnp.mean(o * o, axis=-1, keepdims=True) + EPS)

    def attend(shift_by_max, groups):
        fed = []
        done = {}
        for ci, (rb, g, keep_lo) in enumerate(chains):
            fed.append([t for c in groups[ci] for t in conv_chunk(c)])
            rows = slice(rb * sub_rows, (rb + 1) * sub_rows)
            sl = slice(g * LANES, (g + 1) * LANES)
            zero = zero_after(fed[ci - conv_lag] if ci >= conv_lag else [])
            done[keep_lo] = one_head(q[rows, sl], keep_lo, zero, shift_by_max)
            if not keep_lo:
                og = jnp.where(lo, done[True], done[False])
                o_ref[0, rows, sl] = (og * g_ref[:, sl]).astype(o_ref.dtype)

    if shift_by_max:
        per_block = n_chunks // (tq // sub_rows)
        cpb = len(chains) // (tq // sub_rows)
        groups = [list(range(i // cpb * per_block, (i // cpb + 1) * per_block))
                  if i % cpb == cpb - 1 else [] for i in range(len(chains))]
    else:
        n_fed = len(chains) - conv_lag
        groups = [list(range((n_chunks * i) // n_fed, (n_chunks * (i + 1)) // n_fed))
                  if i < n_fed else [] for i in range(len(chains))]
    attend(shift_by_max, groups)


def _attention(q, kd, vd, gain, u, conv_w, conv_b, weights, *, tq, shift_by_max):
    b, s, _ = q.shape
    nj = s // tq
    n_steps = b * KV_HEADS * nj
    cw = CONV_WIDTH // KV_HEADS
    tile = lambda bi, hi, ji: (bi, ji, hi)
    per_bh = lambda bi, hi, ji: (bi, 0, hi)
    per_h = lambda bi, hi, ji: (0, hi)
    slab = lambda bi, hi, ji: ((bi * KV_HEADS + hi) * nj + ji, 0)
    w_specs = [pl.BlockSpec((w.shape[0] // n_steps, w.shape[1]), slab) for w in weights]
    kern = functools.partial(_attention_kernel, sub_rows=SUB_ROWS, rows_per_chunk=CONV_CHUNK_ROWS,
                             n_cast=len(weights), conv_lag=CONV_LAG_CHAINS,
                             shift_by_max=shift_by_max)
    outs = pl.pallas_call(
        kern,
        grid=(b, KV_HEADS, nj),
        in_specs=[
            pl.BlockSpec((1, tq, 2 * LANES), tile),
            pl.BlockSpec((1, s, LANES), per_bh),
            pl.BlockSpec((1, s, 2 * LANES), per_bh),
            pl.BlockSpec((1, 2 * LANES), per_h),
            pl.BlockSpec((1, s, cw), per_bh),
            pl.BlockSpec((CONV_K, cw), per_h),
            pl.BlockSpec((1, cw), per_h),
        ] + w_specs,
        out_specs=[
            pl.BlockSpec((1, tq, 2 * LANES), tile),
            pl.BlockSpec((1, tq, cw), tile),
        ] + w_specs,
        out_shape=[
            jax.ShapeDtypeStruct((b, s, ATTN_WIDTH), BF16),
            jax.ShapeDtypeStruct((b, s, CONV_WIDTH), F32),
        ] + [jax.ShapeDtypeStruct(w.shape, BF16) for w in weights],
        scratch_shapes=[
            pltpu.VMEM((s + 2 * CONV_HALO, cw), F32),
            pltpu.VMEM((CONV_K, F32_SUBLANES, cw), F32),
        ],
        compiler_params=pltpu.CompilerParams(
            dimension_semantics=("arbitrary", "arbitrary", "arbitrary"),
            vmem_limit_bytes=VMEM_LIMIT),
        name="attention",
    )(q, kd, vd, gain, u, conv_w, conv_b, *weights)
    return outs[0], outs[1], tuple(outs[2:])


def _seg_mean_sq_lanes(z):
    lo = lax.broadcasted_iota(jnp.int32, z.shape, 1) < HEAD_DIM
    s = z * z
    s_lo = jnp.sum(jnp.where(lo, s, 0.0), axis=-1, keepdims=True)
    s_hi = jnp.sum(jnp.where(lo, 0.0, s), axis=-1, keepdims=True)
    return jnp.where(lo, s_lo, s_hi) * (1.0 / HEAD_DIM)


def _mix_out_kernel(c_ref, a_ref, h_ref, lg_ref, lb_ref, cog_ref,
                    wo_ref, g1_ref, b1_ref, o_ref, *, sub_rows):
    subs =[slice(st * sub_rows, (st + 1) * sub_rows) for st in range(a_ref.shape[1] // sub_rows)]
    ycs = []
    for rows in subs:
        c = _layernorm(c_ref[0, rows, :], lg_ref[...], lb_ref[...])
        c = c * jax.nn.sigmoid(c)
        parts = []
        for g in range(CONV_WIDTH // LANES):
            sl = slice(g * LANES, (g + 1) * LANES)
            cg = c[:, sl]
            parts.append((cg * lax.rsqrt(_seg_mean_sq_lanes(cg) + EPS) * cog_ref[:, sl]).astype(BF16))
        ycs.append(jnp.concatenate(parts, axis=1))
    mixes = []
    for rows, yc in zip(subs, ycs):
        mixes.append(jnp.dot(a_ref[0, rows, :], wo_ref[0:ATTN_WIDTH, :], preferred_element_type=F32)
                     + jnp.dot(yc, wo_ref[ATTN_WIDTH:D_MODEL, :], preferred_element_type=F32))
    for rows, mix in zip(subs, mixes):
        o_ref[0, rows, :] = _layernorm(ALPHA * h_ref[0, rows, :] + mix, g1_ref[...], b1_ref[...])


def _mix_out(conv, attn_n, h, ln_g, ln_b, out_g, w_out, g1, b1, *, ts):
    b, s, d = h.shape
    tile = lambda bi, ji: (bi, ji, 0)
    const = lambda bi, ji: (0, 0)
    return pl.pallas_call(
        functools.partial(_mix_out_kernel, sub_rows=SUB_ROWS),
        grid=(b, s // ts),
        in_specs=[
            pl.BlockSpec((1, ts, CONV_WIDTH), tile),
            pl.BlockSpec((1, ts, ATTN_WIDTH), tile),
            pl.BlockSpec((1, ts, d), tile),
            pl.BlockSpec((1, CONV_WIDTH), const),
            pl.BlockSpec((1, CONV_WIDTH), const),
            pl.BlockSpec((1, CONV_WIDTH), const),
            pl.BlockSpec((d, d), const, pipeline_mode=pl.Buffered(1)),
            pl.BlockSpec((1, d), const),
            pl.BlockSpec((1, d), const),
        ],
        out_specs=pl.BlockSpec((1, ts, d), tile),
        out_shape=jax.ShapeDtypeStruct((b, s, d), F32),
        compiler_params=pltpu.CompilerParams(
            dimension_semantics=("arbitrary", "arbitrary"), vmem_limit_bytes=VMEM_LIMIT),
        name="mix_out",
    )(conv, attn_n, h, ln_g, ln_b, out_g, w_out, g1, b1)


def _mem_attn_kernel(h_ref, wq_ref, k_ref, v_ref, wo_ref, g_ref, b_ref, o_ref, *, sub_rows):
    subs = [slice(st * sub_rows, (st + 1) * sub_rows) for st in range(h_ref.shape[1] // sub_rows)]
    heads = [slice(hd * MEM_HEAD_DIM, (hd + 1) * MEM_HEAD_DIM) for hd in range(MEM_HEADS)]
    scores = []
    for rows in subs:
        hb = h_ref[0, rows, :].astype(BF16)
        q = jnp.dot(hb, wq_ref[...], preferred_element_type=F32) * (MEM_HEAD_DIM ** -0.5)
        qb = q.astype(BF16)
        scores.append([lax.dot_general(qb[:, sl], k_ref[0, :, sl], (((1,), (1,)), ((), ())),
                                       preferred_element_type=F32) for sl in heads])
    atts = []
    for sc in scores:
        outs = []
        for s, sl in zip(sc, heads):
            m = jnp.max(s, axis=-1, keepdims=True)
            p = jnp.exp(s - m)
            l = jnp.sum(p, axis=-1, keepdims=True)
            o = jnp.dot(p.astype(BF16), v_ref[0, :, sl], preferred_element_type=F32) / l
            outs.append(o.astype(BF16))
        atts.append(jnp.dot(jnp.concatenate(outs, axis=1), wo_ref[...], preferred_element_type=F32))
    for rows, att in zip(subs, atts):
        o_ref[0, rows, :] = _layernorm(ALPHA * h_ref[0, rows, :] + att, g_ref[...], b_ref[...])


def _mem_attn(h1, wq, km, vm, wo, g2, b2, *, ts):
    b, s, d = h1.shape
    nm = km.shape[1]
    tile = lambda bi, ji: (bi, ji, 0)
    const = lambda bi, ji: (0, 0)
    per_b = lambda bi, ji: (bi, 0, 0)
    return pl.pallas_call(
        functools.partial(_mem_attn_kernel, sub_rows=SUB_ROWS),
        grid=(b, s // ts),
        in_specs=[
            pl.BlockSpec((1, ts, d), tile),
            pl.BlockSpec((d, d), const, pipeline_mode=pl.Buffered(1)),
            pl.BlockSpec((1, nm, d), per_b),
            pl.BlockSpec((1, nm, d), per_b),
            pl.BlockSpec((d, d), const, pipeline_mode=pl.Buffered(1)),
            pl.BlockSpec((1, d), const),
            pl.BlockSpec((1, d), const),
        ],
        out_specs=pl.BlockSpec((1, ts, d), tile),
        out_shape=jax.ShapeDtypeStruct((b, s, d), F32),
        compiler_params=pltpu.CompilerParams(
            dimension_semantics=("arbitrary", "arbitrary"), vmem_limit_bytes=VMEM_LIMIT),
        name="mem_attn",
    )(h1, wq, km, vm, wo, g2, b2)


def _mlp_kernel(h_ref, w1_ref, b1_ref, w2_ref, b2_ref, g_ref, b_ref, o_ref, *, ff_chunk, sub_rows):
    for st in range(h_ref.shape[0] // sub_rows):
        rows = slice(st * sub_rows, (st + 1) * sub_rows)
        h = h_ref[rows, :]
        hb = h.astype(BF16)
        acc = ALPHA * h + b2_ref[...]
        for f in range(D_FF // ff_chunk):
            sl = slice(f * ff_chunk, (f + 1) * ff_chunk)
            a = jnp.dot(hb, w1_ref[:, sl], preferred_element_type=F32) + b1_ref[:, sl]
            a = jnp.maximum(a, 0.0)
            acc = acc + jnp.dot((a * a).astype(BF16), w2_ref[sl, :], preferred_element_type=F32)
        o_ref[rows, :] = _layernorm(acc, g_ref[...], b_ref[...])


def _mlp(h2, w1, b1, w2, b2, g3, b3, *, tm):
    m, d = h2.shape
    row = lambda i: (i, 0)
    const = lambda i: (0, 0)
    kern = functools.partial(_mlp_kernel, ff_chunk=MLP_FF_CHUNK, sub_rows=SUB_ROWS)
    return pl.pallas_call(
        kern,
        grid=(m // tm,),
        in_specs=[
            pl.BlockSpec((tm, d), row),
            pl.BlockSpec((d, D_FF), const, pipeline_mode=pl.Buffered(1)),
            pl.BlockSpec((1, D_FF), const),
            pl.BlockSpec((D_FF, d), const, pipeline_mode=pl.Buffered(1)),
            pl.BlockSpec((1, d), const),
            pl.BlockSpec((1, d), const),
            pl.BlockSpec((1, d), const),
        ],
        out_specs=pl.BlockSpec((tm, d), row),
        out_shape=jax.ShapeDtypeStruct((m, d), F32),
        compiler_params=pltpu.CompilerParams(
            dimension_semantics=("arbitrary",), vmem_limit_bytes=VMEM_LIMIT),
        name="mlp",
    )(h2, w1, b1, w2, b2, g3, b3)


def _rope_tables(seq_len, gain, scale):
    rows = seq_len // GRID_W
    row_ids = jnp.repeat(jnp.arange(rows, dtype=jnp.int32), GRID_W)
    col_ids = jnp.tile(jnp.arange(GRID_W, dtype=jnp.int32), rows)
    inv = ROPE_THETA ** (-jnp.arange(0, AXIS_DIM, 2, dtype=jnp.float32) / AXIS_DIM)
    ang = jnp.concatenate([row_ids[:, None].astype(jnp.float32) * inv,
                           col_ids[:, None].astype(jnp.float32) * inv], axis=-1)
    cos = jnp.repeat(jnp.cos(ang), 2, axis=-1)
    sin = jnp.repeat(jnp.sin(ang), 2, axis=-1)
    sign = jnp.where(jnp.arange(HEAD_DIM) % 2 == 0, -1.0, 1.0).astype(jnp.float32)
    gain_sw = gain.reshape(HEAD_DIM // 2, 2)[:, ::-1].reshape(HEAD_DIM)
    c = cos * gain * scale
    s = sin * sign * gain_sw * scale
    return jnp.tile(c, (1, LANES // HEAD_DIM)), jnp.tile(s, (1, LANES // HEAD_DIM))


def kernel(x, mem, ln_in_g, ln_in_b, w_in, q_norm_g, k_norm_g, conv_w, conv_b, conv_ln_g, conv_ln_b,
           attn_out_g, conv_out_g, w_out, ln1_g, ln1_b, w_mem_q, w_mem_kv, w_mem_o, ln2_g, ln2_b,
           w_ff1, b_ff1, w_ff2, b_ff2, ln3_g, ln3_b):
    b, s, d = x.shape
    assert d == D_MODEL and w_in.shape[0] == DEPTH == 1
    assert s % GRID_W == 0
    m = b * s
    nm = mem.shape[1]
    row = lambda a: a.reshape(1, -1)

    cq, sq = _rope_tables(s, q_norm_g[0], HEAD_DIM ** -0.5 * LOG2_E)
    ck, sk = _rope_tables(s, k_norm_g[0], 1.0)
    seg = jnp.arange(LANES) // HEAD_DIM
    bd = (seg[:, None] == seg[None, :]).astype(BF16)

    h, q, kd, vd, u = _in_proj(x.reshape(m, d), row(ln_in_g), row(ln_in_b), w_in[0].astype(BF16),
                               cq, sq, ck, sk, bd, seq=s, tm=IN_PROJ_ROWS)
    km, vm = _mem_kv(mem.reshape(b * nm, d), w_mem_kv[0], tm=MEM_KV_ROWS)

    gain_max = jnp.max(jnp.abs(jnp.stack([q_norm_g[0], k_norm_g[0]])), axis=1)
    score_bound = HEAD_DIM * gain_max[0] * gain_max[1] * (HEAD_DIM ** -0.5 * LOG2_E)
    attn_n, conv, (w_out_b, w_mq_b, w_mo_b, w_ff1_b, w_ff2_b) = lax.cond(
        score_bound <= SAFE_EXP2_BOUND,
        functools.partial(_attention, tq=ATTN_Q_ROWS, shift_by_max=False),
        functools.partial(_attention, tq=ATTN_Q_ROWS, shift_by_max=True),
        q.reshape(b, s, ATTN_WIDTH), kd.reshape(b, s, 2 * LANES), vd.reshape(b, s, 4 * LANES),
        row(attn_out_g[0]), u.reshape(b, s, CONV_WIDTH), conv_w[0], row(conv_b[0]),
        (w_out[0], w_mem_q[0], w_mem_o[0], w_ff1[0], w_ff2[0]))

    h1 = _mix_out(conv, attn_n, h.reshape(b, s, d), row(conv_ln_g[0]), row(conv_ln_b[0]),
                  row(conv_out_g[0]), w_out_b, row(ln1_g[0]), row(ln1_b[0]), ts=MIX_ROWS)

    h2 = _mem_attn(h1, w_mq_b, km.reshape(b, nm, d), vm.reshape(b, nm, d), w_mo_b,
                   row(ln2_g[0]), row(ln2_b[0]), ts=MEM_ATTN_ROWS)

    out = _mlp(h2.reshape(m, d), w_ff1_b, row(b_ff1[0]), w_ff2_b, row(b_ff2[0]),
               row(ln3_g[0]), row(ln3_b[0]), tm=MLP_ROWS)
    return out.reshape(b, s, d)
```

```python
import functools

import jax
import jax.numpy as jnp
from jax import lax
from jax.experimental import pallas as pl
from jax.experimental.pallas import tpu as pltpu

D_MODEL = 1024
HEAD_DIM = 64
ATTN_HEADS = 8
KV_HEADS = 2
ATTN_WIDTH = ATTN_HEADS * HEAD_DIM
KV_WIDTH = KV_HEADS * HEAD_DIM
CONV_WIDTH = D_MODEL - ATTN_WIDTH
CONV_K = 31
CONV_HALO = 16
MEM_HEADS = 4
MEM_HEAD_DIM = D_MODEL // MEM_HEADS
D_FF = 4 * D_MODEL
GRID_W = 64
AXIS_DIM = HEAD_DIM // 2
ROPE_THETA = 10000.0
EPS = 1e-5
DEPTH = 1
ALPHA = (2 * DEPTH) ** 0.25
LOG2_E = 1.4426950408889634

LANES = 128
F32_SUBLANES = 8
VMEM_LIMIT = 56 * 1024 * 1024

IN_PROJ_ROWS = 1024
MEM_KV_ROWS = 1024
ATTN_Q_ROWS = 1024
MIX_ROWS = 1024
MEM_ATTN_ROWS = 1024
MLP_ROWS = 1024
SUB_ROWS = 256
CONV_CHUNK_ROWS = 32
CONV_LAG_CHAINS = 3
SAFE_EXP2_BOUND = 60.0
MLP_FF_CHUNK = 1024

F32 = jnp.float32
BF16 = jnp.bfloat16


def _layernorm(z, g, b):
    mu = jnp.mean(z, axis=-1, keepdims=True)
    zc = z - mu
    var = jnp.mean(zc * zc, axis=-1, keepdims=True)
    return zc * lax.rsqrt(var + EPS) * g + b


def _seg_mean_sq(z, bd):
    s = z * z
    hi = s.astype(BF16)
    lo = (s - hi.astype(F32)).astype(BF16)
    tot = (jnp.dot(hi, bd, preferred_element_type=F32)
           + jnp.dot(lo, bd, preferred_element_type=F32))
    return tot * (1.0 / HEAD_DIM)


def _in_proj_kernel(x_ref, g_ref, b_ref, w_ref, cq_ref, sq_ref, ck_ref, sk_ref, bd_ref,
                    h_ref, q_ref, kd_ref, vd_ref, u_ref, *, sub_rows):
    bd = bd_ref[...]
    lane = lax.broadcasted_iota(jnp.int32, (sub_rows, LANES), 1)
    even = (lane & 1) == 0
    lo = lane < HEAD_DIM
    ones = jnp.ones((sub_rows, LANES), BF16)
    c0 = ATTN_WIDTH + 2 * KV_WIDTH

    def norm_rope(z, c, s):
        sw = jnp.where(even, pltpu.roll(z, LANES - 1, 1), pltpu.roll(z, 1, 1))
        r = lax.rsqrt(_seg_mean_sq(z, bd) + EPS)
        return r * (z * c + sw * s)

    subs = [slice(st * sub_rows, (st + 1) * sub_rows) for st in range(x_ref.shape[0] // sub_rows)]
    hbs = []
    for rows in subs:
        h = _layernorm(x_ref[rows, :], g_ref[...], b_ref[...])
        h_ref[rows, :] = h
        hbs.append(h.astype(BF16))
    projs = []
    for hb in hbs:
        projs.append((
            jnp.dot(hb, w_ref[:, 0:ATTN_WIDTH], preferred_element_type=F32),
            jnp.dot(hb, w_ref[:, ATTN_WIDTH:c0], preferred_element_type=F32),
            jnp.dot(hb, w_ref[:, c0:c0 + CONV_WIDTH], preferred_element_type=F32),
            jnp.dot(hb, w_ref[:, c0 + CONV_WIDTH:c0 + 2 * CONV_WIDTH], preferred_element_type=F32)))
    for rows, (zq, zkv, val, gate) in zip(subs, projs):
        u_ref[rows, :] = val * jax.nn.sigmoid(gate)
        cq = cq_ref[rows, :]
        sq = sq_ref[rows, :]
        for g in range(ATTN_WIDTH // LANES):
            sl = slice(g * LANES, (g + 1) * LANES)
            q_ref[rows, sl] = norm_rope(zq[:, sl], cq, sq).astype(BF16)
        kk = norm_rope(zkv[:, 0:LANES], ck_ref[rows, :], sk_ref[rows, :])
        kr = pltpu.roll(kk, HEAD_DIM, 1)
        kd_ref[rows, 0:LANES] = jnp.where(lo, kk, kr).astype(BF16)
        kd_ref[rows, LANES:2 * LANES] = jnp.where(lo, kr, kk).astype(BF16)
        vv = zkv[:, LANES:2 * LANES]
        vr = pltpu.roll(vv, HEAD_DIM, 1)
        vd_ref[rows, 0:LANES] = jnp.where(lo, vv, vr).astype(BF16)
        vd_ref[rows, LANES:2 * LANES] = ones
        vd_ref[rows, 2 * LANES:3 * LANES] = jnp.where(lo, vr, vv).astype(BF16)
        vd_ref[rows, 3 * LANES:4 * LANES] = ones


def _in_proj(x2, ln_g, ln_b, w_in, cq, sq, ck, sk, bd, *, seq, tm):
    m, d = x2.shape
    n_in = w_in.shape[1]
    ns = seq // tm
    row = lambda i: (i, 0)
    const = lambda i: (0, 0)
    tab = lambda i: (i % ns, 0)
    return pl.pallas_call(
        functools.partial(_in_proj_kernel, sub_rows=SUB_ROWS),
        grid=(m // tm,),
        in_specs=[
            pl.BlockSpec((tm, d), row),
            pl.BlockSpec((1, d), const),
            pl.BlockSpec((1, d), const),
            pl.BlockSpec((d, n_in), const, pipeline_mode=pl.Buffered(1)),
            pl.BlockSpec((tm, LANES), tab),
            pl.BlockSpec((tm, LANES), tab),
            pl.BlockSpec((tm, LANES), tab),
            pl.BlockSpec((tm, LANES), tab),
            pl.BlockSpec((LANES, LANES), const),
        ],
        out_specs=[
            pl.BlockSpec((tm, d), row),
            pl.BlockSpec((tm, ATTN_WIDTH), row),
            pl.BlockSpec((tm, 2 * LANES), row),
            pl.BlockSpec((tm, 4 * LANES), row),
            pl.BlockSpec((tm, CONV_WIDTH), row),
        ],
        out_shape=[
            jax.ShapeDtypeStruct((m, d), F32),
            jax.ShapeDtypeStruct((m, ATTN_WIDTH), BF16),
            jax.ShapeDtypeStruct((m, 2 * LANES), BF16),
            jax.ShapeDtypeStruct((m, 4 * LANES), BF16),
            jax.ShapeDtypeStruct((m, CONV_WIDTH), F32),
        ],
        compiler_params=pltpu.CompilerParams(
            dimension_semantics=("arbitrary",), vmem_limit_bytes=VMEM_LIMIT),
        name="in_proj",
    )(x2, ln_g, ln_b, w_in, cq, sq, ck, sk, bd)


def _mem_kv_kernel(m_ref, w_ref, k_ref, v_ref):
    mb = m_ref[...].astype(BF16)
    wk = w_ref[:, 0:D_MODEL].astype(BF16)
    wv = w_ref[:, D_MODEL:2 * D_MODEL].astype(BF16)
    k_ref[...] = jnp.dot(mb, wk, preferred_element_type=F32).astype(BF16)
    v_ref[...] = jnp.dot(mb, wv, preferred_element_type=F32).astype(BF16)


def _mem_kv(mem2, w_kv, *, tm):
    m, d = mem2.shape
    row = lambda i: (i, 0)
    return pl.pallas_call(
        _mem_kv_kernel,
        grid=(m // tm,),
        in_specs=[pl.BlockSpec((tm, d), row),
                  pl.BlockSpec((d, 2 * d), lambda i: (0, 0), pipeline_mode=pl.Buffered(1))],
        out_specs=[pl.BlockSpec((tm, d), row), pl.BlockSpec((tm, d), row)],
        out_shape=[jax.ShapeDtypeStruct((m, d), BF16), jax.ShapeDtypeStruct((m, d), BF16)],
        compiler_params=pltpu.CompilerParams(
            dimension_semantics=("arbitrary",), vmem_limit_bytes=VMEM_LIMIT),
        name="mem_kv",
    )(mem2, w_kv)


def _conv_chunk(upad_ref, wb_ref, bias, row0, rc, prev, never):
    sub = F32_SUBLANES
    wrows = rc + 2 * CONV_HALO
    off0 = CONV_HALO - CONV_K // 2
    win = upad_ref[pl.ds(row0, wrows), :]
    shifted = [win] + [pltpu.roll(win, wrows - r, 0) for r in range(1, sub)]
    tiles = []
    for i in range(rc // sub):
        acc = bias if prev is None else jnp.where(never, prev, bias)
        for r in range(sub):
            for a in range((2 * CONV_HALO) // sub):
                t = sub * a + r - off0
                if 0 <= t < CONV_K:
                    lo_row = sub * (a + i)
                    acc = acc + shifted[r][lo_row:lo_row + sub, :] * wb_ref[t]
        tiles.append(acc)
        prev = acc
    return tiles


def _attention_kernel(q_ref, k_ref, v_ref, g_ref, u_ref, cw_ref, cb_ref, *rest,
                      sub_rows, rows_per_chunk, n_cast, conv_lag, shift_by_max):
    w_refs = rest[:n_cast]
    o_ref, c_ref = rest[n_cast:n_cast + 2]
    wb16_refs = rest[n_cast + 2:2 * n_cast + 2]
    upad_ref, wb_ref = rest[2 * n_cast + 2:]
    for w_ref, wb16_ref in zip(w_refs, wb16_refs):
        wb16_ref[...] = w_ref[...].astype(wb16_ref.dtype)
    tq = q_ref.shape[1]
    s = u_ref.shape[1]
    cw = u_ref.shape[2]
    j = pl.program_id(2)

    @pl.when(j == 0)
    def _():
        zeros = jnp.zeros((CONV_HALO, cw), F32)
        upad_ref[0:CONV_HALO, :] = zeros
        upad_ref[CONV_HALO + s:2 * CONV_HALO + s, :] = zeros
        upad_ref[CONV_HALO:CONV_HALO + s, :] = u_ref[0]
        for t in range(CONV_K):
            wb_ref[t] = jnp.broadcast_to(cw_ref[t:t + 1, :], (F32_SUBLANES, cw))

    q = q_ref[0]
    k = k_ref[0]
    v = v_ref[0]
    lane = lax.broadcasted_iota(jnp.int32, (sub_rows, LANES), 1)
    lo = lane < HEAD_DIM
    never = lane >= LANES + j
    bias = jnp.broadcast_to(cb_ref[...], (F32_SUBLANES, cw))
    rc = rows_per_chunk
    chains = [(rb, g, keep_lo) for rb in range(tq // sub_rows) for g in range(2)
              for keep_lo in (True, False)]
    n_chunks = tq // rc

    never_tile = lax.broadcasted_iota(jnp.int32, (F32_SUBLANES, cw), 1) >= cw + j
    last_tile = [None]

    def conv_chunk(ci):
        r0 = ci * rc
        tiles = _conv_chunk(upad_ref, wb_ref, bias, pl.multiple_of(j * tq + r0, rc), rc,
                            last_tile[0], never_tile)
        last_tile[0] = tiles[-1]
        c_ref[0, r0:r0 + rc, :] = jnp.concatenate(tiles, axis=0)
        return tiles

    def zero_after(tiles):
        z = jnp.zeros((sub_rows, LANES), F32)
        if tiles:
            t = functools.reduce(lambda x, y: x + y, tiles)
            t = functools.reduce(lambda x, y: x + y,
                                 [t[:, c * LANES:(c + 1) * LANES] for c in range(cw // LANES)])
            z = jnp.where(never, jnp.broadcast_to(t[0:1, :], z.shape), z)
        return z.astype(q.dtype)

    def one_head(qg, keep_lo, zero, shift_by_max):
        lhs = jnp.where(lo, qg, zero) if keep_lo else jnp.where(lo, zero, qg)
        sc = lax.dot_general(lhs, k, (((1,), (1,)), ((), ())), preferred_element_type=F32)
        if shift_by_max:
            sc = sc - jnp.max(sc, axis=-1, keepdims=True)
        p = jnp.exp2(sc)
        if shift_by_max:
            p = p.astype(BF16)
        r = lax.dot_general(p, v, (((1,), (0,)), ((), ())),
                            preferred_element_type=F32)
        o = r[:, 0:LANES] / r[:, LANES:2 * LANES]
        return o * lax.rsqrt(jnp.mean(o * o, axis=-1, keepdims=True) + EPS)

    def attend(shift_by_max, groups):
        fed = []
        done = {}
        for ci, (rb, g, keep_lo) in enumerate(chains):
            fed.append([t for c in groups[ci] for t in conv_chunk(c)])
            rows = slice(rb * sub_rows, (rb + 1) * sub_rows)
            sl = slice(g * LANES, (g + 1) * LANES)
            zero = zero_after(fed[ci - conv_lag] if ci >= conv_lag else [])
            done[keep_lo] = one_head(q[rows, sl], keep_lo, zero, shift_by_max)
            if not keep_lo:
                og = jnp.where(lo, done[True], done[False])
                o_ref[0, rows, sl] = (og * g_ref[:, sl]).astype(o_ref.dtype)

    if shift_by_max:
        per_block = n_chunks // (tq // sub_rows)
        cpb = len(chains) // (tq // sub_rows)
        groups = [list(range(i // cpb * per_block, (i // cpb + 1) * per_block))
                  if i % cpb == cpb - 1 else [] for i in range(len(chains))]
    else:
        n_fed = len(chains) - conv_lag
        groups = [list(range((n_chunks * i) // n_fed, (n_chunks * (i + 1)) // n_fed))
                  if i < n_fed else [] for i in range(len(chains))]
    attend(shift_by_max, groups)


def _attention(q, kd, vd, gain, u, conv_w, conv_b, weights, *, tq, shift_by_max):
    b, s, _ = q.shape
    nj = s // tq
    n_steps = b * KV_HEADS * nj
    cw = CONV_WIDTH // KV_HEADS
    tile = lambda bi, hi, ji: (bi, ji, hi)
    per_bh = lambda bi, hi, ji: (bi, 0, hi)
    per_h = lambda bi, hi, ji: (0, hi)
    slab = lambda bi, hi, ji: ((bi * KV_HEADS + hi) * nj + ji, 0)
    w_specs = [pl.BlockSpec((w.shape[0] // n_steps, w.shape[1]), slab) for w in weights]
    kern = functools.partial(_attention_kernel, sub_rows=SUB_ROWS, rows_per_chunk=CONV_CHUNK_ROWS,
                             n_cast=len(weights), conv_lag=CONV_LAG_CHAINS,
                             shift_by_max=shift_by_max)
    outs = pl.pallas_call(
        kern,
        grid=(b, KV_HEADS, nj),
        in_specs=[
            pl.BlockSpec((1, tq, 2 * LANES), tile),
            pl.BlockSpec((1, s, LANES), per_bh),
            pl.BlockSpec((1, s, 2 * LANES), per_bh),
            pl.BlockSpec((1, 2 * LANES), per_h),
            pl.BlockSpec((1, s, cw), per_bh),
            pl.BlockSpec((CONV_K, cw), per_h),
            pl.BlockSpec((1, cw), per_h),
        ] + w_specs,
        out_specs=[
            pl.BlockSpec((1, tq, 2 * LANES), tile),
            pl.BlockSpec((1, tq, cw), tile),
        ] + w_specs,
        out_shape=[
            jax.ShapeDtypeStruct((b, s, ATTN_WIDTH), BF16),
            jax.ShapeDtypeStruct((b, s, CONV_WIDTH), F32),
        ] + [jax.ShapeDtypeStruct(w.shape, BF16) for w in weights],
        scratch_shapes=[
            pltpu.VMEM((s + 2 * CONV_HALO, cw), F32),
            pltpu.VMEM((CONV_K, F32_SUBLANES, cw), F32),
        ],
        compiler_params=pltpu.CompilerParams(
            dimension_semantics=("arbitrary", "arbitrary", "arbitrary"),
            vmem_limit_bytes=VMEM_LIMIT),
        name="attention",
    )(q, kd, vd, gain, u, conv_w, conv_b, *weights)
    return outs[0], outs[1], tuple(outs[2:])


def _seg_mean_sq_lanes(z):
    lo = lax.broadcasted_iota(jnp.int32, z.shape, 1) < HEAD_DIM
    s = z * z
    s_lo = jnp.sum(jnp.where(lo, s, 0.0), axis=-1, keepdims=True)
    s_hi = jnp.sum(jnp.where(lo, 0.0, s), axis=-1, keepdims=True)
    return jnp.where(lo, s_lo, s_hi) * (1.0 / HEAD_DIM)


def _mix_out_kernel(c_ref, a_ref, h_ref, lg_ref, lb_ref, cog_ref,
                    wo_ref, g1_ref, b1_ref, o_ref, *, sub_rows):
    subs =[slice(st * sub_rows, (st + 1) * sub_rows) for st in range(a_ref.shape[1] // sub_rows)]
    ycs = []
    for rows in subs:
        c = _layernorm(c_ref[0, rows, :], lg_ref[...], lb_ref[...])
        c = c * jax.nn.sigmoid(c)
        parts = []
        for g in range(CONV_WIDTH // LANES):
            sl = slice(g * LANES, (g + 1) * LANES)
            cg = c[:, sl]
            parts.append((cg * lax.rsqrt(_seg_mean_sq_lanes(cg) + EPS) * cog_ref[:, sl]).astype(BF16))
        ycs.append(jnp.concatenate(parts, axis=1))
    mixes = []
    for rows, yc in zip(subs, ycs):
        mixes.append(jnp.dot(a_ref[0, rows, :], wo_ref[0:ATTN_WIDTH, :], preferred_element_type=F32)
                     + jnp.dot(yc, wo_ref[ATTN_WIDTH:D_MODEL, :], preferred_element_type=F32))
    for rows, mix in zip(subs, mixes):
        o_ref[0, rows, :] = _layernorm(ALPHA * h_ref[0, rows, :] + mix, g1_ref[...], b1_ref[...])


def _mix_out(conv, attn_n, h, ln_g, ln_b, out_g, w_out, g1, b1, *, ts):
    b, s, d = h.shape
    tile = lambda bi, ji: (bi, ji, 0)
    const = lambda bi, ji: (0, 0)
    return pl.pallas_call(
        functools.partial(_mix_out_kernel, sub_rows=SUB_ROWS),
        grid=(b, s // ts),
        in_specs=[
            pl.BlockSpec((1, ts, CONV_WIDTH), tile),
            pl.BlockSpec((1, ts, ATTN_WIDTH), tile),
            pl.BlockSpec((1, ts, d), tile),
            pl.BlockSpec((1, CONV_WIDTH), const),
            pl.BlockSpec((1, CONV_WIDTH), const),
            pl.BlockSpec((1, CONV_WIDTH), const),
            pl.BlockSpec((d, d), const, pipeline_mode=pl.Buffered(1)),
            pl.BlockSpec((1, d), const),
            pl.BlockSpec((1, d), const),
        ],
        out_specs=pl.BlockSpec((1, ts, d), tile),
        out_shape=jax.ShapeDtypeStruct((b, s, d), F32),
        compiler_params=pltpu.CompilerParams(
            dimension_semantics=("arbitrary", "arbitrary"), vmem_limit_bytes=VMEM_LIMIT),
        name="mix_out",
    )(conv, attn_n, h, ln_g, ln_b, out_g, w_out, g1, b1)


def _mem_attn_kernel(h_ref, wq_ref, k_ref, v_ref, wo_ref, g_ref, b_ref, o_ref, *, sub_rows):
    subs = [slice(st * sub_rows, (st + 1) * sub_rows) for st in range(h_ref.shape[1] // sub_rows)]
    heads = [slice(hd * MEM_HEAD_DIM, (hd + 1) * MEM_HEAD_DIM) for hd in range(MEM_HEADS)]
    scores = []
    for rows in subs:
        hb = h_ref[0, rows, :].astype(BF16)
        q = jnp.dot(hb, wq_ref[...], preferred_element_type=F32) * (MEM_HEAD_DIM ** -0.5)
        qb = q.astype(BF16)
        scores.append([lax.dot_general(qb[:, sl], k_ref[0, :, sl], (((1,), (1,)), ((), ())),
                                       preferred_element_type=F32) for sl in heads])
    atts = []
    for sc in scores:
        outs = []
        for s, sl in zip(sc, heads):
            m = jnp.max(s, axis=-1, keepdims=True)
            p = jnp.exp(s - m)
            l = jnp.sum(p, axis=-1, keepdims=True)
            o = jnp.dot(p.astype(BF16), v_ref[0, :, sl], preferred_element_type=F32) / l
            outs.append(o.astype(BF16))
        atts.append(jnp.dot(jnp.concatenate(outs, axis=1), wo_ref[...], preferred_element_type=F32))
    for rows, att in zip(subs, atts):
        o_ref[0, rows, :] = _layernorm(ALPHA * h_ref[0, rows, :] + att, g_ref[...], b_ref[...])


def _mem_attn(h1, wq, km, vm, wo, g2, b2, *, ts):
    b, s, d = h1.shape
    nm = km.shape[1]
    tile = lambda bi, ji: (bi, ji, 0)
    const = lambda bi, ji: (0, 0)
    per_b = lambda bi, ji: (bi, 0, 0)
    return pl.pallas_call(
        functools.partial(_mem_attn_kernel, sub_rows=SUB_ROWS),
        grid=(b, s // ts),
        in_specs=[
            pl.BlockSpec((1, ts, d), tile),
            pl.BlockSpec((d, d), const, pipeline_mode=pl.Buffered(1)),
            pl.BlockSpec((1, nm, d), per_b),
            pl.BlockSpec((1, nm, d), per_b),
            pl.BlockSpec((d, d), const, pipeline_mode=pl.Buffered(1)),
            pl.BlockSpec((1, d), const),
            pl.BlockSpec((1, d), const),
        ],
        out_specs=pl.BlockSpec((1, ts, d), tile),
        out_shape=jax.ShapeDtypeStruct((b, s, d), F32),
        compiler_params=pltpu.CompilerParams(
            dimension_semantics=("arbitrary", "arbitrary"), vmem_limit_bytes=VMEM_LIMIT),
        name="mem_attn",
    )(h1, wq, km, vm, wo, g2, b2)


def _mlp_kernel(h_ref, w1_ref, b1_ref, w2_ref, b2_ref, g_ref, b_ref, o_ref, *, ff_chunk, sub_rows):
    subs = [slice(st * sub_rows, (st + 1) * sub_rows) for st in range(h_ref.shape[0] // sub_rows)]
    hs = [h_ref[rows, :] for rows in subs]
    hbs = [h.astype(BF16) for h in hs]
    accs = [ALPHA * h + b2_ref[...] for h in hs]
    for f in range(D_FF // ff_chunk):
        sl = slice(f * ff_chunk, (f + 1) * ff_chunk)
        acts = []
        for hb in hbs:
            a = jnp.maximum(jnp.dot(hb, w1_ref[:, sl], preferred_element_type=F32) + b1_ref[:, sl], 0.0)
            acts.append((a * a).astype(BF16))
        accs = [acc + jnp.dot(act, w2_ref[sl, :], preferred_element_type=F32)
                for acc, act in zip(accs, acts)]
    for rows, acc in zip(subs, accs):
        o_ref[rows, :] = _layernorm(acc, g_ref[...], b_ref[...])


def _mlp(h2, w1, b1, w2, b2, g3, b3, *, tm):
    m, d = h2.shape
    row = lambda i: (i, 0)
    const = lambda i: (0, 0)
    kern = functools.partial(_mlp_kernel, ff_chunk=MLP_FF_CHUNK, sub_rows=SUB_ROWS)
    return pl.pallas_call(
        kern,
        grid=(m // tm,),
        in_specs=[
            pl.BlockSpec((tm, d), row),
            pl.BlockSpec((d, D_FF), const, pipeline_mode=pl.Buffered(1)),
            pl.BlockSpec((1, D_FF), const),
            pl.BlockSpec((D_FF, d), const, pipeline_mode=pl.Buffered(1)),
            pl.BlockSpec((1, d), const),
            pl.BlockSpec((1, d), const),
            pl.BlockSpec((1, d), const),
        ],
        out_specs=pl.BlockSpec((tm, d), row),
        out_shape=jax.ShapeDtypeStruct((m, d), F32),
        compiler_params=pltpu.CompilerParams(
            dimension_semantics=("arbitrary",), vmem_limit_bytes=VMEM_LIMIT),
        name="mlp",
    )(h2, w1, b1, w2, b2, g3, b3)


def _rope_tables(seq_len, gain, scale):
    rows = seq_len // GRID_W
    row_ids = jnp.repeat(jnp.arange(rows, dtype=jnp.int32), GRID_W)
    col_ids = jnp.tile(jnp.arange(GRID_W, dtype=jnp.int32), rows)
    inv = ROPE_THETA ** (-jnp.arange(0, AXIS_DIM, 2, dtype=jnp.float32) / AXIS_DIM)
    ang = jnp.concatenate([row_ids[:, None].astype(jnp.float32) * inv,
                           col_ids[:, None].astype(jnp.float32) * inv], axis=-1)
    cos = jnp.repeat(jnp.cos(ang), 2, axis=-1)
    sin = jnp.repeat(jnp.sin(ang), 2, axis=-1)
    sign = jnp.where(jnp.arange(HEAD_DIM) % 2 == 0, -1.0, 1.0).astype(jnp.float32)
    gain_sw = gain.reshape(HEAD_DIM // 2, 2)[:, ::-1].reshape(HEAD_DIM)
    c = cos * gain * scale
    s = sin * sign * gain_sw * scale
    return jnp.tile(c, (1, LANES // HEAD_DIM)), jnp.tile(s, (1, LANES // HEAD_DIM))


def kernel(x, mem, ln_in_g, ln_in_b, w_in, q_norm_g, k_norm_g, conv_w, conv_b, conv_ln_g, conv_ln_b,
           attn_out_g, conv_out_g, w_out, ln1_g, ln1_b, w_mem_q, w_mem_kv, w_mem_o, ln2_g, ln2_b,
           w_ff1, b_ff1, w_ff2, b_ff2, ln3_g, ln3_b):
    b, s, d = x.shape
    assert d == D_MODEL and w_in.shape[0] == DEPTH == 1
    assert s % GRID_W == 0
    m = b * s
    nm = mem.shape[1]
    row = lambda a: a.reshape(1, -1)

    cq, sq = _rope_tables(s, q_norm_g[0], HEAD_DIM ** -0.5 * LOG2_E)
    ck, sk = _rope_tables(s, k_norm_g[0], 1.0)
    seg = jnp.arange(LANES) // HEAD_DIM
    bd = (seg[:, None] == seg[None, :]).astype(BF16)

    h, q, kd, vd, u = _in_proj(x.reshape(m, d), row(ln_in_g), row(ln_in_b), w_in[0].astype(BF16),
                               cq, sq, ck, sk, bd, seq=s, tm=IN_PROJ_ROWS)
    km, vm = _mem_kv(mem.reshape(b * nm, d), w_mem_kv[0], tm=MEM_KV_ROWS)

    gain_max = jnp.max(jnp.abs(jnp.stack([q_norm_g[0], k_norm_g[0]])), axis=1)
    score_bound = HEAD_DIM * gain_max[0] * gain_max[1] * (HEAD_DIM ** -0.5 * LOG2_E)
    attn_n, conv, (w_out_b, w_mq_b, w_mo_b, w_ff1_b, w_ff2_b) = lax.cond(
        score_bound <= SAFE_EXP2_BOUND,
        functools.partial(_attention, tq=ATTN_Q_ROWS, shift_by_max=False),
        functools.partial(_attention, tq=ATTN_Q_ROWS, shift_by_max=True),
        q.reshape(b, s, ATTN_WIDTH), kd.reshape(b, s, 2 * LANES), vd.reshape(b, s, 4 * LANES),
        row(attn_out_g[0]), u.reshape(b, s, CONV_WIDTH), conv_w[0], row(conv_b[0]),
        (w_out[0], w_mem_q[0], w_mem_o[0], w_ff1[0], w_ff2[0]))

    h1 = _mix_out(conv, attn_n, h.reshape(b, s, d), row(conv_ln_g[0]), row(conv_ln_b[0]),
                  row(conv_out_g[0]), w_out_b, row(ln1_g[0]), row(ln1_b[0]), ts=MIX_ROWS)

    h2 = _mem_attn(h1, w_mq_b, km.reshape(b, nm, d), vm.reshape(b, nm, d), w_mo_b,
                   row(ln2_g[0]), row(ln2_b[0]), ts=MEM_ATTN_ROWS)

    out = _mlp(h2.reshape(m, d), w_ff1_b, row(b_ff1[0]), w_ff2_b, row(b_ff2[0]),
               row(ln3_g[0]), row(ln3_b[0]), tm=MLP_ROWS)
    return out.reshape(b, s, d)
```

```python
import functools

import jax
import jax.numpy as jnp
from jax import lax
from jax.experimental import pallas as pl
from jax.experimental.pallas import tpu as pltpu

D_MODEL = 1024
HEAD_DIM = 64
ATTN_HEADS = 8
KV_HEADS = 2
ATTN_WIDTH = ATTN_HEADS * HEAD_DIM
KV_WIDTH = KV_HEADS * HEAD_DIM
CONV_WIDTH = D_MODEL - ATTN_WIDTH
CONV_K = 31
CONV_HALO = 16
MEM_HEADS = 4
MEM_HEAD_DIM = D_MODEL // MEM_HEADS
D_FF = 4 * D_MODEL
GRID_W = 64
AXIS_DIM = HEAD_DIM // 2
ROPE_THETA = 10000.0
EPS = 1e-5
DEPTH = 1
ALPHA = (2 * DEPTH) ** 0.25
LOG2_E = 1.4426950408889634

LANES = 128
F32_SUBLANES = 8
VMEM_LIMIT = 56 * 1024 * 1024

IN_PROJ_ROWS = 1024
MEM_KV_ROWS = 1024
ATTN_Q_ROWS = 1024
MIX_ROWS = 1024
MEM_ATTN_ROWS = 1024
MLP_ROWS = 1024
SUB_ROWS = 256
CONV_CHUNK_ROWS = 64
CONV_LAG_CHAINS = 3
SAFE_EXP2_BOUND = 60.0
MLP_FF_CHUNK = 1024

F32 = jnp.float32
BF16 = jnp.bfloat16


def _layernorm(z, g, b):
    mu = jnp.mean(z, axis=-1, keepdims=True)
    zc = z - mu
    var = jnp.mean(zc * zc, axis=-1, keepdims=True)
    return zc * lax.rsqrt(var + EPS) * g + b


def _seg_mean_sq(z, bd):
    s = z * z
    hi = s.astype(BF16)
    lo = (s - hi.astype(F32)).astype(BF16)
    tot = (jnp.dot(hi, bd, preferred_element_type=F32)
           + jnp.dot(lo, bd, preferred_element_type=F32))
    return tot * (1.0 / HEAD_DIM)


def _in_proj_kernel(x_ref, g_ref, b_ref, w_ref, cq_ref, sq_ref, ck_ref, sk_ref, bd_ref,
                    h_ref, q_ref, kd_ref, vd_ref, u_ref, *, sub_rows):
    bd = bd_ref[...]
    lane = lax.broadcasted_iota(jnp.int32, (sub_rows, LANES), 1)
    even = (lane & 1) == 0
    lo = lane < HEAD_DIM
    ones = jnp.ones((sub_rows, LANES), BF16)
    c0 = ATTN_WIDTH + 2 * KV_WIDTH

    def norm_rope(z, c, s):
        sw = jnp.where(even, pltpu.roll(z, LANES - 1, 1), pltpu.roll(z, 1, 1))
        r = lax.rsqrt(_seg_mean_sq(z, bd) + EPS)
        return r * (z * c + sw * s)

    subs = [slice(st * sub_rows, (st + 1) * sub_rows) for st in range(x_ref.shape[0] // sub_rows)]
    hbs = []
    for rows in subs:
        h = _layernorm(x_ref[rows, :], g_ref[...], b_ref[...])
        h_ref[rows, :] = h
        hbs.append(h.astype(BF16))
    projs = []
    for hb in hbs:
        projs.append((
            jnp.dot(hb, w_ref[:, 0:ATTN_WIDTH], preferred_element_type=F32),
            jnp.dot(hb, w_ref[:, ATTN_WIDTH:c0], preferred_element_type=F32),
            jnp.dot(hb, w_ref[:, c0:c0 + CONV_WIDTH], preferred_element_type=F32),
            jnp.dot(hb, w_ref[:, c0 + CONV_WIDTH:c0 + 2 * CONV_WIDTH], preferred_element_type=F32)))
    for rows, (zq, zkv, val, gate) in zip(subs, projs):
        u_ref[rows, :] = val * jax.nn.sigmoid(gate)
        cq = cq_ref[rows, :]
        sq = sq_ref[rows, :]
        for g in range(ATTN_WIDTH // LANES):
            sl = slice(g * LANES, (g + 1) * LANES)
            q_ref[rows, sl] = norm_rope(zq[:, sl], cq, sq).astype(BF16)
        kk = norm_rope(zkv[:, 0:LANES], ck_ref[rows, :], sk_ref[rows, :])
        kr = pltpu.roll(kk, HEAD_DIM, 1)
        kd_ref[rows, 0:LANES] = jnp.where(lo, kk, kr).astype(BF16)
        kd_ref[rows, LANES:2 * LANES] = jnp.where(lo, kr, kk).astype(BF16)
        vv = zkv[:, LANES:2 * LANES]
        vr = pltpu.roll(vv, HEAD_DIM, 1)
        vd_ref[rows, 0:LANES] = jnp.where(lo, vv, vr).astype(BF16)
        vd_ref[rows, LANES:2 * LANES] = ones
        vd_ref[rows, 2 * LANES:3 * LANES] = jnp.where(lo, vr, vv).astype(BF16)
        vd_ref[rows, 3 * LANES:4 * LANES] = ones


def _in_proj(x2, ln_g, ln_b, w_in, cq, sq, ck, sk, bd, *, seq, tm):
    m, d = x2.shape
    n_in = w_in.shape[1]
    ns = seq // tm
    row = lambda i: (i, 0)
    const = lambda i: (0, 0)
    tab = lambda i: (i % ns, 0)
    return pl.pallas_call(
        functools.partial(_in_proj_kernel, sub_rows=SUB_ROWS),
        grid=(m // tm,),
        in_specs=[
            pl.BlockSpec((tm, d), row),
            pl.BlockSpec((1, d), const),
            pl.BlockSpec((1, d), const),
            pl.BlockSpec((d, n_in), const, pipeline_mode=pl.Buffered(1)),
            pl.BlockSpec((tm, LANES), tab),
            pl.BlockSpec((tm, LANES), tab),
            pl.BlockSpec((tm, LANES), tab),
            pl.BlockSpec((tm, LANES), tab),
            pl.BlockSpec((LANES, LANES), const),
        ],
        out_specs=[
            pl.BlockSpec((tm, d), row),
            pl.BlockSpec((tm, ATTN_WIDTH), row),
            pl.BlockSpec((tm, 2 * LANES), row),
            pl.BlockSpec((tm, 4 * LANES), row),
            pl.BlockSpec((tm, CONV_WIDTH), row),
        ],
        out_shape=[
            jax.ShapeDtypeStruct((m, d), F32),
            jax.ShapeDtypeStruct((m, ATTN_WIDTH), BF16),
            jax.ShapeDtypeStruct((m, 2 * LANES), BF16),
            jax.ShapeDtypeStruct((m, 4 * LANES), BF16),
            jax.ShapeDtypeStruct((m, CONV_WIDTH), F32),
        ],
        compiler_params=pltpu.CompilerParams(
            dimension_semantics=("arbitrary",), vmem_limit_bytes=VMEM_LIMIT),
        name="in_proj",
    )(x2, ln_g, ln_b, w_in, cq, sq, ck, sk, bd)


def _mem_kv_kernel(m_ref, w_ref, k_ref, v_ref):
    mb = m_ref[...].astype(BF16)
    wk = w_ref[:, 0:D_MODEL].astype(BF16)
    wv = w_ref[:, D_MODEL:2 * D_MODEL].astype(BF16)
    k_ref[...] = jnp.dot(mb, wk, preferred_element_type=F32).astype(BF16)
    v_ref[...] = jnp.dot(mb, wv, preferred_element_type=F32).astype(BF16)


def _mem_kv(mem2, w_kv, *, tm):
    m, d = mem2.shape
    row = lambda i: (i, 0)
    return pl.pallas_call(
        _mem_kv_kernel,
        grid=(m // tm,),
        in_specs=[pl.BlockSpec((tm, d), row),
                  pl.BlockSpec((d, 2 * d), lambda i: (0, 0), pipeline_mode=pl.Buffered(1))],
        out_specs=[pl.BlockSpec((tm, d), row), pl.BlockSpec((tm, d), row)],
        out_shape=[jax.ShapeDtypeStruct((m, d), BF16), jax.ShapeDtypeStruct((m, d), BF16)],
        compiler_params=pltpu.CompilerParams(
            dimension_semantics=("arbitrary",), vmem_limit_bytes=VMEM_LIMIT),
        name="mem_kv",
    )(mem2, w_kv)


def _conv_chunk(upad_ref, wb_ref, bias, row0, rc, prev, never):
    sub = F32_SUBLANES
    wrows = rc + 2 * CONV_HALO
    off0 = CONV_HALO - CONV_K // 2
    win = upad_ref[pl.ds(row0, wrows), :]
    shifted = [win] + [pltpu.roll(win, wrows - r, 0) for r in range(1, sub)]
    tiles = []
    for i in range(rc // sub):
        acc = bias if prev is None else jnp.where(never, prev, bias)
        for r in range(sub):
            for a in range((2 * CONV_HALO) // sub):
                t = sub * a + r - off0
                if 0 <= t < CONV_K:
                    lo_row = sub * (a + i)
                    acc = acc + shifted[r][lo_row:lo_row + sub, :] * wb_ref[t]
        tiles.append(acc)
        prev = acc
    return tiles


def _attention_kernel(q_ref, k_ref, v_ref, g_ref, u_ref, cw_ref, cb_ref, *rest,
                      sub_rows, rows_per_chunk, n_cast, conv_lag, shift_by_max):
    w_refs = rest[:n_cast]
    o_ref, c_ref = rest[n_cast:n_cast + 2]
    wb16_refs = rest[n_cast + 2:2 * n_cast + 2]
    upad_ref, wb_ref = rest[2 * n_cast + 2:]
    for w_ref, wb16_ref in zip(w_refs, wb16_refs):
        wb16_ref[...] = w_ref[...].astype(wb16_ref.dtype)
    tq = q_ref.shape[1]
    s = u_ref.shape[1]
    cw = u_ref.shape[2]
    j = pl.program_id(2)

    @pl.when(j == 0)
    def _():
        zeros = jnp.zeros((CONV_HALO, cw), F32)
        upad_ref[0:CONV_HALO, :] = zeros
        upad_ref[CONV_HALO + s:2 * CONV_HALO + s, :] = zeros
        upad_ref[CONV_HALO:CONV_HALO + s, :] = u_ref[0]
        for t in range(CONV_K):
            wb_ref[t] = jnp.broadcast_to(cw_ref[t:t + 1, :], (F32_SUBLANES, cw))

    q = q_ref[0]
    k = k_ref[0]
    v = v_ref[0]
    lane = lax.broadcasted_iota(jnp.int32, (sub_rows, LANES), 1)
    lo = lane < HEAD_DIM
    never = lane >= LANES + j
    bias = jnp.broadcast_to(cb_ref[...], (F32_SUBLANES, cw))
    rc = rows_per_chunk
    chains = [(rb, g, keep_lo) for rb in range(tq // sub_rows) for g in range(2)
              for keep_lo in (True, False)]
    n_chunks = tq // rc

    never_tile = lax.broadcasted_iota(jnp.int32, (F32_SUBLANES, cw), 1) >= cw + j
    last_tile = [None]

    def conv_chunk(ci):
        r0 = ci * rc
        tiles = _conv_chunk(upad_ref, wb_ref, bias, pl.multiple_of(j * tq + r0, rc), rc,
                            last_tile[0], never_tile)
        last_tile[0] = tiles[-1]
        c_ref[0, r0:r0 + rc, :] = jnp.concatenate(tiles, axis=0)
        return tiles

    def zero_after(tiles):
        z = jnp.zeros((sub_rows, LANES), F32)
        if tiles:
            t = functools.reduce(lambda x, y: x + y, tiles)
            t = functools.reduce(lambda x, y: x + y,
                                 [t[:, c * LANES:(c + 1) * LANES] for c in range(cw // LANES)])
            z = jnp.where(never, jnp.broadcast_to(t[0:1, :], z.shape), z)
        return z.astype(q.dtype)

    def one_head(qg, keep_lo, zero, shift_by_max):
        lhs = jnp.where(lo, qg, zero) if keep_lo else jnp.where(lo, zero, qg)
        sc = lax.dot_general(lhs, k, (((1,), (1,)), ((), ())), preferred_element_type=F32)
        if shift_by_max:
            sc = sc - jnp.max(sc, axis=-1, keepdims=True)
        p = jnp.exp2(sc)
        if shift_by_max:
            p = p.astype(BF16)
        r = lax.dot_general(p, v, (((1,), (0,)), ((), ())),
                            preferred_element_type=F32)
        o = r[:, 0:LANES] / r[:, LANES:2 * LANES]
        return o * lax.rsqrt(jnp.mean(o * o, axis=-1, keepdims=True) + EPS)

    def attend(shift_by_max, groups):
        fed = []
        done = {}
        for ci, (rb, g, keep_lo) in enumerate(chains):
            fed.append([t for c in groups[ci] for t in conv_chunk(c)])
            rows = slice(rb * sub_rows, (rb + 1) * sub_rows)
            sl = slice(g * LANES, (g + 1) * LANES)
            zero = zero_after(fed[ci - conv_lag] if ci >= conv_lag else [])
            done[keep_lo] = one_head(q[rows, sl], keep_lo, zero, shift_by_max)
            if not keep_lo:
                og = jnp.where(lo, done[True], done[False])
                o_ref[0, rows, sl] = (og * g_ref[:, sl]).astype(o_ref.dtype)

    if shift_by_max:
        per_block = n_chunks // (tq // sub_rows)
        cpb = len(chains) // (tq // sub_rows)
        groups = [list(range(i // cpb * per_block, (i // cpb + 1) * per_block))
                  if i % cpb == cpb - 1 else [] for i in range(len(chains))]
    else:
        n_fed = len(chains) - conv_lag
        groups = [list(range((n_chunks * i) // n_fed, (n_chunks * (i + 1)) // n_fed))
                  if i < n_fed else [] for i in range(len(chains))]
    attend(shift_by_max, groups)


def _attention(q, kd, vd, gain, u, conv_w, conv_b, weights, *, tq, shift_by_max):
    b, s, _ = q.shape
    nj = s // tq
    n_steps = b * KV_HEADS * nj
    cw = CONV_WIDTH // KV_HEADS
    tile = lambda bi, hi, ji: (bi, ji, hi)
    per_bh = lambda bi, hi, ji: (bi, 0, hi)
    per_h = lambda bi, hi, ji: (0, hi)
    slab = lambda bi, hi, ji: ((bi * KV_HEADS + hi) * nj + ji, 0)
    w_specs = [pl.BlockSpec((w.shape[0] // n_steps, w.shape[1]), slab) for w in weights]
    kern = functools.partial(_attention_kernel, sub_rows=SUB_ROWS, rows_per_chunk=CONV_CHUNK_ROWS,
                             n_cast=len(weights), conv_lag=CONV_LAG_CHAINS,
                             shift_by_max=shift_by_max)
    outs = pl.pallas_call(
        kern,
        grid=(b, KV_HEADS, nj),
        in_specs=[
            pl.BlockSpec((1, tq, 2 * LANES), tile),
            pl.BlockSpec((1, s, LANES), per_bh),
            pl.BlockSpec((1, s, 2 * LANES), per_bh),
            pl.BlockSpec((1, 2 * LANES), per_h),
            pl.BlockSpec((1, s, cw), per_bh),
            pl.BlockSpec((CONV_K, cw), per_h),
            pl.BlockSpec((1, cw), per_h),
        ] + w_specs,
        out_specs=[
            pl.BlockSpec((1, tq, 2 * LANES), tile),
            pl.BlockSpec((1, tq, cw), tile),
        ] + w_specs,
        out_shape=[
            jax.ShapeDtypeStruct((b, s, ATTN_WIDTH), BF16),
            jax.ShapeDtypeStruct((b, s, CONV_WIDTH), F32),
        ] + [jax.ShapeDtypeStruct(w.shape, BF16) for w in weights],
        scratch_shapes=[
            pltpu.VMEM((s + 2 * CONV_HALO, cw), F32),
            pltpu.VMEM((CONV_K, F32_SUBLANES, cw), F32),
        ],
        compiler_params=pltpu.CompilerParams(
            dimension_semantics=("arbitrary", "arbitrary", "arbitrary"),
            vmem_limit_bytes=VMEM_LIMIT),
        name="attention",
    )(q, kd, vd, gain, u, conv_w, conv_b, *weights)
    return outs[0], outs[1], tuple(outs[2:])


def _seg_mean_sq_lanes(z):
    lo = lax.broadcasted_iota(jnp.int32, z.shape, 1) < HEAD_DIM
    s = z * z
    s_lo = jnp.sum(jnp.where(lo, s, 0.0), axis=-1, keepdims=True)
    s_hi = jnp.sum(jnp.where(lo, 0.0, s), axis=-1, keepdims=True)
    return jnp.where(lo, s_lo, s_hi) * (1.0 / HEAD_DIM)


def _mix_out_kernel(c_ref, a_ref, h_ref, lg_ref, lb_ref, cog_ref,
                    wo_ref, g1_ref, b1_ref, o_ref, *, sub_rows):
    subs =[slice(st * sub_rows, (st + 1) * sub_rows) for st in range(a_ref.shape[1] // sub_rows)]
    ycs = []
    for rows in subs:
        c = _layernorm(c_ref[0, rows, :], lg_ref[...], lb_ref[...])
        c = c * jax.nn.sigmoid(c)
        parts = []
        for g in range(CONV_WIDTH // LANES):
            sl = slice(g * LANES, (g + 1) * LANES)
            cg = c[:, sl]
            parts.append((cg * lax.rsqrt(_seg_mean_sq_lanes(cg) + EPS) * cog_ref[:, sl]).astype(BF16))
        ycs.append(jnp.concatenate(parts, axis=1))
    mixes = []
    for rows, yc in zip(subs, ycs):
        mixes.append(jnp.dot(a_ref[0, rows, :], wo_ref[0:ATTN_WIDTH, :], preferred_element_type=F32)
                     + jnp.dot(yc, wo_ref[ATTN_WIDTH:D_MODEL, :], preferred_element_type=F32))
    for rows, mix in zip(subs, mixes):
        o_ref[0, rows, :] = _layernorm(ALPHA * h_ref[0, rows, :] + mix, g1_ref[...], b1_ref[...])


def _mix_out(conv, attn_n, h, ln_g, ln_b, out_g, w_out, g1, b1, *, ts):
    b, s, d = h.shape
    tile = lambda bi, ji: (bi, ji, 0)
    const = lambda bi, ji: (0, 0)
    return pl.pallas_call(
        functools.partial(_mix_out_kernel, sub_rows=SUB_ROWS),
        grid=(b, s // ts),
        in_specs=[
            pl.BlockSpec((1, ts, CONV_WIDTH), tile),
            pl.BlockSpec((1, ts, ATTN_WIDTH), tile),
            pl.BlockSpec((1, ts, d), tile),
            pl.BlockSpec((1, CONV_WIDTH), const),
            pl.BlockSpec((1, CONV_WIDTH), const),
            pl.BlockSpec((1, CONV_WIDTH), const),
            pl.BlockSpec((d, d), const, pipeline_mode=pl.Buffered(1)),
            pl.BlockSpec((1, d), const),
            pl.BlockSpec((1, d), const),
        ],
        out_specs=pl.BlockSpec((1, ts, d), tile),
        out_shape=jax.ShapeDtypeStruct((b, s, d), F32),
        compiler_params=pltpu.CompilerParams(
            dimension_semantics=("arbitrary", "arbitrary"), vmem_limit_bytes=VMEM_LIMIT),
        name="mix_out",
    )(conv, attn_n, h, ln_g, ln_b, out_g, w_out, g1, b1)


def _mem_attn_kernel(h_ref, wq_ref, k_ref, v_ref, wo_ref, g_ref, b_ref, o_ref, *, sub_rows):
    subs = [slice(st * sub_rows, (st + 1) * sub_rows) for st in range(h_ref.shape[1] // sub_rows)]
    heads = [slice(hd * MEM_HEAD_DIM, (hd + 1) * MEM_HEAD_DIM) for hd in range(MEM_HEADS)]
    scores = []
    for rows in subs:
        hb = h_ref[0, rows, :].astype(BF16)
        q = jnp.dot(hb, wq_ref[...], preferred_element_type=F32) * (MEM_HEAD_DIM ** -0.5)
        qb = q.astype(BF16)
        scores.append([lax.dot_general(qb[:, sl], k_ref[0, :, sl], (((1,), (1,)), ((), ())),
                                       preferred_element_type=F32) for sl in heads])
    atts = []
    for sc in scores:
        outs = []
        for s, sl in zip(sc, heads):
            m = jnp.max(s, axis=-1, keepdims=True)
            p = jnp.exp(s - m)
            l = jnp.sum(p, axis=-1, keepdims=True)
            o = jnp.dot(p.astype(BF16), v_ref[0, :, sl], preferred_element_type=F32) / l
            outs.append(o.astype(BF16))
        atts.append(jnp.dot(jnp.concatenate(outs, axis=1), wo_ref[...], preferred_element_type=F32))
    for rows, att in zip(subs, atts):
        o_ref[0, rows, :] = _layernorm(ALPHA * h_ref[0, rows, :] + att, g_ref[...], b_ref[...])


def _mem_attn(h1, wq, km, vm, wo, g2, b2, *, ts):
    b, s, d = h1.shape
    nm = km.shape[1]
    tile = lambda bi, ji: (bi, ji, 0)
    const = lambda bi, ji: (0, 0)
    per_b = lambda bi, ji: (bi, 0, 0)
    return pl.pallas_call(
        functools.partial(_mem_attn_kernel, sub_rows=SUB_ROWS),
        grid=(b, s // ts),
        in_specs=[
            pl.BlockSpec((1, ts, d), tile),
            pl.BlockSpec((d, d), const, pipeline_mode=pl.Buffered(1)),
            pl.BlockSpec((1, nm, d), per_b),
            pl.BlockSpec((1, nm, d), per_b),
            pl.BlockSpec((d, d), const, pipeline_mode=pl.Buffered(1)),
            pl.BlockSpec((1, d), const),
            pl.BlockSpec((1, d), const),
        ],
        out_specs=pl.BlockSpec((1, ts, d), tile),
        out_shape=jax.ShapeDtypeStruct((b, s, d), F32),
        compiler_params=pltpu.CompilerParams(
            dimension_semantics=("arbitrary", "arbitrary"), vmem_limit_bytes=VMEM_LIMIT),
        name="mem_attn",
    )(h1, wq, km, vm, wo, g2, b2)


def _mlp_kernel(h_ref, w1_ref, b1_ref, w2_ref, b2_ref, g_ref, b_ref, o_ref, *, ff_chunk, sub_rows):
    subs = [slice(st * sub_rows, (st + 1) * sub_rows) for st in range(h_ref.shape[0] // sub_rows)]
    hs = [h_ref[rows, :] for rows in subs]
    hbs = [h.astype(BF16) for h in hs]
    accs = [ALPHA * h + b2_ref[...] for h in hs]
    for f in range(D_FF // ff_chunk):
        sl = slice(f * ff_chunk, (f + 1) * ff_chunk)
        acts = []
        for hb in hbs:
            a = jnp.maximum(jnp.dot(hb, w1_ref[:, sl], preferred_element_type=F32) + b1_ref[:, sl], 0.0)
            acts.append((a * a).astype(BF16))
        accs = [acc + jnp.dot(act, w2_ref[sl, :], preferred_element_type=F32)
                for acc, act in zip(accs, acts)]
    for rows, acc in zip(subs, accs):
        o_ref[rows, :] = _layernorm(acc, g_ref[...], b_ref[...])


def _mlp(h2, w1, b1, w2, b2, g3, b3, *, tm):
    m, d = h2.shape
    row = lambda i: (i, 0)
    const = lambda i: (0, 0)
    kern = functools.partial(_mlp_kernel, ff_chunk=MLP_FF_CHUNK, sub_rows=SUB_ROWS)
    return pl.pallas_call(
        kern,
        grid=(m // tm,),
        in_specs=[
            pl.BlockSpec((tm, d), row),
            pl.BlockSpec((d, D_FF), const, pipeline_mode=pl.Buffered(1)),
            pl.BlockSpec((1, D_FF), const),
            pl.BlockSpec((D_FF, d), const, pipeline_mode=pl.Buffered(1)),
            pl.BlockSpec((1, d), const),
            pl.BlockSpec((1, d), const),
            pl.BlockSpec((1, d), const),
        ],
        out_specs=pl.BlockSpec((tm, d), row),
        out_shape=jax.ShapeDtypeStruct((m, d), F32),
        compiler_params=pltpu.CompilerParams(
            dimension_semantics=("arbitrary",), vmem_limit_bytes=VMEM_LIMIT),
        name="mlp",
    )(h2, w1, b1, w2, b2, g3, b3)


def _rope_tables(seq_len, gain, scale):
    rows = seq_len // GRID_W
    row_ids = jnp.repeat(jnp.arange(rows, dtype=jnp.int32), GRID_W)
    col_ids = jnp.tile(jnp.arange(GRID_W, dtype=jnp.int32), rows)
    inv = ROPE_THETA ** (-jnp.arange(0, AXIS_DIM, 2, dtype=jnp.float32) / AXIS_DIM)
    ang = jnp.concatenate([row_ids[:, None].astype(jnp.float32) * inv,
                           col_ids[:, None].astype(jnp.float32) * inv], axis=-1)
    cos = jnp.repeat(jnp.cos(ang), 2, axis=-1)
    sin = jnp.repeat(jnp.sin(ang), 2, axis=-1)
    sign = jnp.where(jnp.arange(HEAD_DIM) % 2 == 0, -1.0, 1.0).astype(jnp.float32)
    gain_sw = gain.reshape(HEAD_DIM // 2, 2)[:, ::-1].reshape(HEAD_DIM)
    c = cos * gain * scale
    s = sin * sign * gain_sw * scale
    return jnp.tile(c, (1, LANES // HEAD_DIM)), jnp.tile(s, (1, LANES // HEAD_DIM))


def kernel(x, mem, ln_in_g, ln_in_b, w_in, q_norm_g, k_norm_g, conv_w, conv_b, conv_ln_g, conv_ln_b,
           attn_out_g, conv_out_g, w_out, ln1_g, ln1_b, w_mem_q, w_mem_kv, w_mem_o, ln2_g, ln2_b,
           w_ff1, b_ff1, w_ff2, b_ff2, ln3_g, ln3_b):
    b, s, d = x.shape
    assert d == D_MODEL and w_in.shape[0] == DEPTH == 1
    assert s % GRID_W == 0
    m = b * s
    nm = mem.shape[1]
    row = lambda a: a.reshape(1, -1)

    cq, sq = _rope_tables(s, q_norm_g[0], HEAD_DIM ** -0.5 * LOG2_E)
    ck, sk = _rope_tables(s, k_norm_g[0], 1.0)
    seg = jnp.arange(LANES) // HEAD_DIM
    bd = (seg[:, None] == seg[None, :]).astype(BF16)

    h, q, kd, vd, u = _in_proj(x.reshape(m, d), row(ln_in_g), row(ln_in_b), w_in[0].astype(BF16),
                               cq, sq, ck, sk, bd, seq=s, tm=IN_PROJ_ROWS)
    km, vm = _mem_kv(mem.reshape(b * nm, d), w_mem_kv[0], tm=MEM_KV_ROWS)

    gain_max = jnp.max(jnp.abs(jnp.stack([q_norm_g[0], k_norm_g[0]])), axis=1)
    score_bound = HEAD_DIM * gain_max[0] * gain_max[1] * (HEAD_DIM ** -0.5 * LOG2_E)
    attn_n, conv, (w_out_b, w_mq_b, w_mo_b, w_ff1_b, w_ff2_b) = lax.cond(
        score_bound <= SAFE_EXP2_BOUND,
        functools.partial(_attention, tq=ATTN_Q_ROWS, shift_by_max=False),
        functools.partial(_attention, tq=ATTN_Q_ROWS, shift_by_max=True),
        q.reshape(b, s, ATTN_WIDTH), kd.reshape(b, s, 2 * LANES), vd.reshape(b, s, 4 * LANES),
        row(attn_out_g[0]), u.reshape(b, s, CONV_WIDTH), conv_w[0], row(conv_b[0]),
        (w_out[0], w_mem_q[0], w_mem_o[0], w_ff1[0], w_ff2[0]))

    h1 = _mix_out(conv, attn_n, h.reshape(b, s, d), row(conv_ln_g[0]), row(conv_ln_b[0]),
                  row(conv_out_g[0]), w_out_b, row(ln1_g[0]), row(ln1_b[0]), ts=MIX_ROWS)

    h2 = _mem_attn(h1, w_mq_b, km.reshape(b, nm, d), vm.reshape(b, nm, d), w_mo_b,
                   row(ln2_g[0]), row(ln2_b[0]), ts=MEM_ATTN_ROWS)

    out = _mlp(h2.reshape(m, d), w_ff1_b, row(b_ff1[0]), w_ff2_b, row(b_ff2[0]),
               row(ln3_g[0]), row(ln3_b[0]), tm=MLP_ROWS)
    return out.reshape(b, s, d)
```

```python
import functools

import jax
import jax.numpy as jnp
from jax import lax
from jax.experimental import pallas as pl
from jax.experimental.pallas import tpu as pltpu

D_MODEL = 1024
HEAD_DIM = 64
ATTN_HEADS = 8
KV_HEADS = 2
ATTN_WIDTH = ATTN_HEADS * HEAD_DIM
KV_WIDTH = KV_HEADS * HEAD_DIM
CONV_WIDTH = D_MODEL - ATTN_WIDTH
CONV_K = 31
CONV_HALO = 16
MEM_HEADS = 4
MEM_HEAD_DIM = D_MODEL // MEM_HEADS
D_FF = 4 * D_MODEL
GRID_W = 64
AXIS_DIM = HEAD_DIM // 2
ROPE_THETA = 10000.0
EPS = 1e-5
DEPTH = 1
ALPHA = (2 * DEPTH) ** 0.25
LOG2_E = 1.4426950408889634

LANES = 128
F32_SUBLANES = 8
VMEM_LIMIT = 56 * 1024 * 1024

IN_PROJ_ROWS = 1024
MEM_KV_ROWS = 1024
ATTN_Q_ROWS = 1024
MIX_ROWS = 1024
MEM_ATTN_ROWS = 1024
MIX_MEM_ROWS = 1024
MLP_ROWS = 1024
SUB_ROWS = 256
CONV_CHUNK_ROWS = 64
CONV_LAG_CHAINS = 3
SAFE_EXP2_BOUND = 60.0
MLP_FF_CHUNK = 1024

F32 = jnp.float32
BF16 = jnp.bfloat16


def _layernorm(z, g, b):
    mu = jnp.mean(z, axis=-1, keepdims=True)
    zc = z - mu
    var = jnp.mean(zc * zc, axis=-1, keepdims=True)
    return zc * lax.rsqrt(var + EPS) * g + b


def _seg_mean_sq(z, bd):
    s = z * z
    hi = s.astype(BF16)
    lo = (s - hi.astype(F32)).astype(BF16)
    tot = (jnp.dot(hi, bd, preferred_element_type=F32)
           + jnp.dot(lo, bd, preferred_element_type=F32))
    return tot * (1.0 / HEAD_DIM)


def _in_proj_kernel(x_ref, g_ref, b_ref, w_ref, cq_ref, sq_ref, ck_ref, sk_ref, bd_ref,
                    h_ref, q_ref, kd_ref, vd_ref, u_ref, *, sub_rows):
    bd = bd_ref[...]
    lane = lax.broadcasted_iota(jnp.int32, (sub_rows, LANES), 1)
    even = (lane & 1) == 0
    lo = lane < HEAD_DIM
    ones = jnp.ones((sub_rows, LANES), BF16)
    c0 = ATTN_WIDTH + 2 * KV_WIDTH

    def norm_rope(z, c, s):
        sw = jnp.where(even, pltpu.roll(z, LANES - 1, 1), pltpu.roll(z, 1, 1))
        r = lax.rsqrt(_seg_mean_sq(z, bd) + EPS)
        return r * (z * c + sw * s)

    subs = [slice(st * sub_rows, (st + 1) * sub_rows) for st in range(x_ref.shape[0] // sub_rows)]
    hbs = []
    for rows in subs:
        h = _layernorm(x_ref[rows, :], g_ref[...], b_ref[...])
        h_ref[rows, :] = h
        hbs.append(h.astype(BF16))
    projs = []
    for hb in hbs:
        projs.append((
            jnp.dot(hb, w_ref[:, 0:ATTN_WIDTH], preferred_element_type=F32),
            jnp.dot(hb, w_ref[:, ATTN_WIDTH:c0], preferred_element_type=F32),
            jnp.dot(hb, w_ref[:, c0:c0 + CONV_WIDTH], preferred_element_type=F32),
            jnp.dot(hb, w_ref[:, c0 + CONV_WIDTH:c0 + 2 * CONV_WIDTH], preferred_element_type=F32)))
    for rows, (zq, zkv, val, gate) in zip(subs, projs):
        u_ref[rows, :] = val * jax.nn.sigmoid(gate)
        cq = cq_ref[rows, :]
        sq = sq_ref[rows, :]
        for g in range(ATTN_WIDTH // LANES):
            sl = slice(g * LANES, (g + 1) * LANES)
            q_ref[rows, sl] = norm_rope(zq[:, sl], cq, sq).astype(BF16)
        kk = norm_rope(zkv[:, 0:LANES], ck_ref[rows, :], sk_ref[rows, :])
        kr = pltpu.roll(kk, HEAD_DIM, 1)
        kd_ref[rows, 0:LANES] = jnp.where(lo, kk, kr).astype(BF16)
        kd_ref[rows, LANES:2 * LANES] = jnp.where(lo, kr, kk).astype(BF16)
        vv = zkv[:, LANES:2 * LANES]
        vr = pltpu.roll(vv, HEAD_DIM, 1)
        vd_ref[rows, 0:LANES] = jnp.where(lo, vv, vr).astype(BF16)
        vd_ref[rows, LANES:2 * LANES] = ones
        vd_ref[rows, 2 * LANES:3 * LANES] = jnp.where(lo, vr, vv).astype(BF16)
        vd_ref[rows, 3 * LANES:4 * LANES] = ones


def _in_proj(x2, ln_g, ln_b, w_in, cq, sq, ck, sk, bd, *, seq, tm):
    m, d = x2.shape
    n_in = w_in.shape[1]
    ns = seq // tm
    row = lambda i: (i, 0)
    const = lambda i: (0, 0)
    tab = lambda i: (i % ns, 0)
    return pl.pallas_call(
        functools.partial(_in_proj_kernel, sub_rows=SUB_ROWS),
        grid=(m // tm,),
        in_specs=[
            pl.BlockSpec((tm, d), row),
            pl.BlockSpec((1, d), const),
            pl.BlockSpec((1, d), const),
            pl.BlockSpec((d, n_in), const, pipeline_mode=pl.Buffered(1)),
            pl.BlockSpec((tm, LANES), tab),
            pl.BlockSpec((tm, LANES), tab),
            pl.BlockSpec((tm, LANES), tab),
            pl.BlockSpec((tm, LANES), tab),
            pl.BlockSpec((LANES, LANES), const),
        ],
        out_specs=[
            pl.BlockSpec((tm, d), row),
            pl.BlockSpec((tm, ATTN_WIDTH), row),
            pl.BlockSpec((tm, 2 * LANES), row),
            pl.BlockSpec((tm, 4 * LANES), row),
            pl.BlockSpec((tm, CONV_WIDTH), row),
        ],
        out_shape=[
            jax.ShapeDtypeStruct((m, d), F32),
            jax.ShapeDtypeStruct((m, ATTN_WIDTH), BF16),
            jax.ShapeDtypeStruct((m, 2 * LANES), BF16),
            jax.ShapeDtypeStruct((m, 4 * LANES), BF16),
            jax.ShapeDtypeStruct((m, CONV_WIDTH), F32),
        ],
        compiler_params=pltpu.CompilerParams(
            dimension_semantics=("arbitrary",), vmem_limit_bytes=VMEM_LIMIT),
        name="in_proj",
    )(x2, ln_g, ln_b, w_in, cq, sq, ck, sk, bd)


def _mem_kv_kernel(m_ref, w_ref, k_ref, v_ref):
    mb = m_ref[...].astype(BF16)
    wk = w_ref[:, 0:D_MODEL].astype(BF16)
    wv = w_ref[:, D_MODEL:2 * D_MODEL].astype(BF16)
    k_ref[...] = jnp.dot(mb, wk, preferred_element_type=F32).astype(BF16)
    v_ref[...] = jnp.dot(mb, wv, preferred_element_type=F32).astype(BF16)


def _mem_kv(mem2, w_kv, *, tm):
    m, d = mem2.shape
    row = lambda i: (i, 0)
    return pl.pallas_call(
        _mem_kv_kernel,
        grid=(m // tm,),
        in_specs=[pl.BlockSpec((tm, d), row),
                  pl.BlockSpec((d, 2 * d), lambda i: (0, 0), pipeline_mode=pl.Buffered(1))],
        out_specs=[pl.BlockSpec((tm, d), row), pl.BlockSpec((tm, d), row)],
        out_shape=[jax.ShapeDtypeStruct((m, d), BF16), jax.ShapeDtypeStruct((m, d), BF16)],
        compiler_params=pltpu.CompilerParams(
            dimension_semantics=("arbitrary",), vmem_limit_bytes=VMEM_LIMIT),
        name="mem_kv",
    )(mem2, w_kv)


def _conv_chunk(upad_ref, wb_ref, bias, row0, rc, prev, never):
    sub = F32_SUBLANES
    wrows = rc + 2 * CONV_HALO
    off0 = CONV_HALO - CONV_K // 2
    win = upad_ref[pl.ds(row0, wrows), :]
    shifted = [win] + [pltpu.roll(win, wrows - r, 0) for r in range(1, sub)]
    tiles = []
    for i in range(rc // sub):
        acc = bias if prev is None else jnp.where(never, prev, bias)
        for r in range(sub):
            for a in range((2 * CONV_HALO) // sub):
                t = sub * a + r - off0
                if 0 <= t < CONV_K:
                    lo_row = sub * (a + i)
                    acc = acc + shifted[r][lo_row:lo_row + sub, :] * wb_ref[t]
        tiles.append(acc)
        prev = acc
    return tiles


def _attention_kernel(q_ref, k_ref, v_ref, g_ref, u_ref, cw_ref, cb_ref, *rest,
                      sub_rows, rows_per_chunk, n_cast, conv_lag, shift_by_max):
    w_refs = rest[:n_cast]
    o_ref, c_ref = rest[n_cast:n_cast + 2]
    wb16_refs = rest[n_cast + 2:2 * n_cast + 2]
    upad_ref, wb_ref = rest[2 * n_cast + 2:]
    for w_ref, wb16_ref in zip(w_refs, wb16_refs):
        wb16_ref[...] = w_ref[...].astype(wb16_ref.dtype)
    tq = q_ref.shape[1]
    s = u_ref.shape[1]
    cw = u_ref.shape[2]
    j = pl.program_id(2)

    @pl.when(j == 0)
    def _():
        zeros = jnp.zeros((CONV_HALO, cw), F32)
        upad_ref[0:CONV_HALO, :] = zeros
        upad_ref[CONV_HALO + s:2 * CONV_HALO + s, :] = zeros
        upad_ref[CONV_HALO:CONV_HALO + s, :] = u_ref[0]
        for t in range(CONV_K):
            wb_ref[t] = jnp.broadcast_to(cw_ref[t:t + 1, :], (F32_SUBLANES, cw))

    q = q_ref[0]
    k = k_ref[0]
    v = v_ref[0]
    lane = lax.broadcasted_iota(jnp.int32, (sub_rows, LANES), 1)
    lo = lane < HEAD_DIM
    never = lane >= LANES + j
    bias = jnp.broadcast_to(cb_ref[...], (F32_SUBLANES, cw))
    rc = rows_per_chunk
    chains = [(rb, g, keep_lo) for rb in range(tq // sub_rows) for g in range(2)
              for keep_lo in (True, False)]
    n_chunks = tq // rc

    never_tile = lax.broadcasted_iota(jnp.int32, (F32_SUBLANES, cw), 1) >= cw + j
    last_tile = [None]

    def conv_chunk(ci):
        r0 = ci * rc
        tiles = _conv_chunk(upad_ref, wb_ref, bias, pl.multiple_of(j * tq + r0, rc), rc,
                            last_tile[0], never_tile)
        last_tile[0] = tiles[-1]
        c_ref[0, r0:r0 + rc, :] = jnp.concatenate(tiles, axis=0)
        return tiles

    def zero_after(tiles):
        z = jnp.zeros((sub_rows, LANES), F32)
        if tiles:
            t = functools.reduce(lambda x, y: x + y, tiles)
            t = functools.reduce(lambda x, y: x + y,
                                 [t[:, c * LANES:(c + 1) * LANES] for c in range(cw // LANES)])
            z = jnp.where(never, jnp.broadcast_to(t[0:1, :], z.shape), z)
        return z.astype(q.dtype)

    def one_head(qg, keep_lo, zero, shift_by_max):
        lhs = jnp.where(lo, qg, zero) if keep_lo else jnp.where(lo, zero, qg)
        sc = lax.dot_general(lhs, k, (((1,), (1,)), ((), ())), preferred_element_type=F32)
        if shift_by_max:
            sc = sc - jnp.max(sc, axis=-1, keepdims=True)
        p = jnp.exp2(sc)
        if shift_by_max:
            p = p.astype(BF16)
        r = lax.dot_general(p, v, (((1,), (0,)), ((), ())),
                            preferred_element_type=F32)
        o = r[:, 0:LANES] / r[:, LANES:2 * LANES]
        return o * lax.rsqrt(jnp.mean(o * o, axis=-1, keepdims=True) + EPS)

    def attend(shift_by_max, groups):
        fed = []
        done = {}
        for ci, (rb, g, keep_lo) in enumerate(chains):
            fed.append([t for c in groups[ci] for t in conv_chunk(c)])
            rows = slice(rb * sub_rows, (rb + 1) * sub_rows)
            sl = slice(g * LANES, (g + 1) * LANES)
            zero = zero_after(fed[ci - conv_lag] if ci >= conv_lag else [])
            done[keep_lo] = one_head(q[rows, sl], keep_lo, zero, shift_by_max)
            if not keep_lo:
                og = jnp.where(lo, done[True], done[False])
                o_ref[0, rows, sl] = (og * g_ref[:, sl]).astype(o_ref.dtype)

    if shift_by_max:
        per_block = n_chunks // (tq // sub_rows)
        cpb = len(chains) // (tq // sub_rows)
        groups = [list(range(i // cpb * per_block, (i // cpb + 1) * per_block))
                  if i % cpb == cpb - 1 else [] for i in range(len(chains))]
    else:
        n_fed = len(chains) - conv_lag
        groups = [list(range((n_chunks * i) // n_fed, (n_chunks * (i + 1)) // n_fed))
                  if i < n_fed else [] for i in range(len(chains))]
    attend(shift_by_max, groups)


def _attention(q, kd, vd, gain, u, conv_w, conv_b, weights, *, tq, shift_by_max):
    b, s, _ = q.shape
    nj = s // tq
    n_steps = b * KV_HEADS * nj
    cw = CONV_WIDTH // KV_HEADS
    tile = lambda bi, hi, ji: (bi, ji, hi)
    per_bh = lambda bi, hi, ji: (bi, 0, hi)
    per_h = lambda bi, hi, ji: (0, hi)
    slab = lambda bi, hi, ji: ((bi * KV_HEADS + hi) * nj + ji, 0)
    w_specs = [pl.BlockSpec((w.shape[0] // n_steps, w.shape[1]), slab) for w in weights]
    kern = functools.partial(_attention_kernel, sub_rows=SUB_ROWS, rows_per_chunk=CONV_CHUNK_ROWS,
                             n_cast=len(weights), conv_lag=CONV_LAG_CHAINS,
                             shift_by_max=shift_by_max)
    outs = pl.pallas_call(
        kern,
        grid=(b, KV_HEADS, nj),
        in_specs=[
            pl.BlockSpec((1, tq, 2 * LANES), tile),
            pl.BlockSpec((1, s, LANES), per_bh),
            pl.BlockSpec((1, s, 2 * LANES), per_bh),
            pl.BlockSpec((1, 2 * LANES), per_h),
            pl.BlockSpec((1, s, cw), per_bh),
            pl.BlockSpec((CONV_K, cw), per_h),
            pl.BlockSpec((1, cw), per_h),
        ] + w_specs,
        out_specs=[
            pl.BlockSpec((1, tq, 2 * LANES), tile),
            pl.BlockSpec((1, tq, cw), tile),
        ] + w_specs,
        out_shape=[
            jax.ShapeDtypeStruct((b, s, ATTN_WIDTH), BF16),
            jax.ShapeDtypeStruct((b, s, CONV_WIDTH), F32),
        ] + [jax.ShapeDtypeStruct(w.shape, BF16) for w in weights],
        scratch_shapes=[
            pltpu.VMEM((s + 2 * CONV_HALO, cw), F32),
            pltpu.VMEM((CONV_K, F32_SUBLANES, cw), F32),
        ],
        compiler_params=pltpu.CompilerParams(
            dimension_semantics=("arbitrary", "arbitrary", "arbitrary"),
            vmem_limit_bytes=VMEM_LIMIT),
        name="attention",
    )(q, kd, vd, gain, u, conv_w, conv_b, *weights)
    return outs[0], outs[1], tuple(outs[2:])


def _seg_mean_sq_lanes(z):
    lo = lax.broadcasted_iota(jnp.int32, z.shape, 1) < HEAD_DIM
    s = z * z
    s_lo = jnp.sum(jnp.where(lo, s, 0.0), axis=-1, keepdims=True)
    s_hi = jnp.sum(jnp.where(lo, 0.0, s), axis=-1, keepdims=True)
    return jnp.where(lo, s_lo, s_hi) * (1.0 / HEAD_DIM)


def _mix_out_kernel(c_ref, a_ref, h_ref, lg_ref, lb_ref, cog_ref,
                    wo_ref, g1_ref, b1_ref, o_ref, *, sub_rows):
    subs =[slice(st * sub_rows, (st + 1) * sub_rows) for st in range(a_ref.shape[1] // sub_rows)]
    ycs = []
    for rows in subs:
        c = _layernorm(c_ref[0, rows, :], lg_ref[...], lb_ref[...])
        c = c * jax.nn.sigmoid(c)
        parts = []
        for g in range(CONV_WIDTH // LANES):
            sl = slice(g * LANES, (g + 1) * LANES)
            cg = c[:, sl]
            parts.append((cg * lax.rsqrt(_seg_mean_sq_lanes(cg) + EPS) * cog_ref[:, sl]).astype(BF16))
        ycs.append(jnp.concatenate(parts, axis=1))
    mixes = []
    for rows, yc in zip(subs, ycs):
        mixes.append(jnp.dot(a_ref[0, rows, :], wo_ref[0:ATTN_WIDTH, :], preferred_element_type=F32)
                     + jnp.dot(yc, wo_ref[ATTN_WIDTH:D_MODEL, :], preferred_element_type=F32))
    for rows, mix in zip(subs, mixes):
        o_ref[0, rows, :] = _layernorm(ALPHA * h_ref[0, rows, :] + mix, g1_ref[...], b1_ref[...])


def _mix_out(conv, attn_n, h, ln_g, ln_b, out_g, w_out, g1, b1, *, ts):
    b, s, d = h.shape
    tile = lambda bi, ji: (bi, ji, 0)
    const = lambda bi, ji: (0, 0)
    return pl.pallas_call(
        functools.partial(_mix_out_kernel, sub_rows=SUB_ROWS),
        grid=(b, s // ts),
        in_specs=[
            pl.BlockSpec((1, ts, CONV_WIDTH), tile),
            pl.BlockSpec((1, ts, ATTN_WIDTH), tile),
            pl.BlockSpec((1, ts, d), tile),
            pl.BlockSpec((1, CONV_WIDTH), const),
            pl.BlockSpec((1, CONV_WIDTH), const),
            pl.BlockSpec((1, CONV_WIDTH), const),
            pl.BlockSpec((d, d), const, pipeline_mode=pl.Buffered(1)),
            pl.BlockSpec((1, d), const),
            pl.BlockSpec((1, d), const),
        ],
        out_specs=pl.BlockSpec((1, ts, d), tile),
        out_shape=jax.ShapeDtypeStruct((b, s, d), F32),
        compiler_params=pltpu.CompilerParams(
            dimension_semantics=("arbitrary", "arbitrary"), vmem_limit_bytes=VMEM_LIMIT),
        name="mix_out",
    )(conv, attn_n, h, ln_g, ln_b, out_g, w_out, g1, b1)


def _mem_attn_kernel(h_ref, wq_ref, k_ref, v_ref, wo_ref, g_ref, b_ref, o_ref, *, sub_rows):
    subs = [slice(st * sub_rows, (st + 1) * sub_rows) for st in range(h_ref.shape[1] // sub_rows)]
    heads = [slice(hd * MEM_HEAD_DIM, (hd + 1) * MEM_HEAD_DIM) for hd in range(MEM_HEADS)]
    scores = []
    for rows in subs:
        hb = h_ref[0, rows, :].astype(BF16)
        q = jnp.dot(hb, wq_ref[...], preferred_element_type=F32) * (MEM_HEAD_DIM ** -0.5)
        qb = q.astype(BF16)
        scores.append([lax.dot_general(qb[:, sl], k_ref[0, :, sl], (((1,), (1,)), ((), ())),
                                       preferred_element_type=F32) for sl in heads])
    atts = []
    for sc in scores:
        outs = []
        for s, sl in zip(sc, heads):
            m = jnp.max(s, axis=-1, keepdims=True)
            p = jnp.exp(s - m)
            l = jnp.sum(p, axis=-1, keepdims=True)
            o = jnp.dot(p.astype(BF16), v_ref[0, :, sl], preferred_element_type=F32) / l
            outs.append(o.astype(BF16))
        atts.append(jnp.dot(jnp.concatenate(outs, axis=1), wo_ref[...], preferred_element_type=F32))
    for rows, att in zip(subs, atts):
        o_ref[0, rows, :] = _layernorm(ALPHA * h_ref[0, rows, :] + att, g_ref[...], b_ref[...])


def _mem_attn(h1, wq, km, vm, wo, g2, b2, *, ts):
    b, s, d = h1.shape
    nm = km.shape[1]
    tile = lambda bi, ji: (bi, ji, 0)
    const = lambda bi, ji: (0, 0)
    per_b = lambda bi, ji: (bi, 0, 0)
    return pl.pallas_call(
        functools.partial(_mem_attn_kernel, sub_rows=SUB_ROWS),
        grid=(b, s // ts),
        in_specs=[
            pl.BlockSpec((1, ts, d), tile),
            pl.BlockSpec((d, d), const, pipeline_mode=pl.Buffered(1)),
            pl.BlockSpec((1, nm, d), per_b),
            pl.BlockSpec((1, nm, d), per_b),
            pl.BlockSpec((d, d), const, pipeline_mode=pl.Buffered(1)),
            pl.BlockSpec((1, d), const),
            pl.BlockSpec((1, d), const),
        ],
        out_specs=pl.BlockSpec((1, ts, d), tile),
        out_shape=jax.ShapeDtypeStruct((b, s, d), F32),
        compiler_params=pltpu.CompilerParams(
            dimension_semantics=("arbitrary", "arbitrary"), vmem_limit_bytes=VMEM_LIMIT),
        name="mem_attn",
    )(h1, wq, km, vm, wo, g2, b2)


def _mix_mem_kernel(c_ref, a_ref, h_ref, lg_ref, lb_ref, cog_ref, wout_ref, g1_ref, b1_ref,
                    wq_ref, k_ref, v_ref, wo_ref, g2_ref, b2_ref, o_ref, *, sub_rows):
    subs = [slice(st * sub_rows, (st + 1) * sub_rows) for st in range(a_ref.shape[1] // sub_rows)]
    heads = [slice(hd * MEM_HEAD_DIM, (hd + 1) * MEM_HEAD_DIM) for hd in range(MEM_HEADS)]
    ycs = []
    for rows in subs:
        c = _layernorm(c_ref[0, rows, :], lg_ref[...], lb_ref[...])
        c = c * jax.nn.sigmoid(c)
        parts = []
        for g in range(CONV_WIDTH // LANES):
            sl = slice(g * LANES, (g + 1) * LANES)
            cg = c[:, sl]
            parts.append((cg * lax.rsqrt(_seg_mean_sq_lanes(cg) + EPS) * cog_ref[:, sl]).astype(BF16))
        ycs.append(jnp.concatenate(parts, axis=1))
    mixes = []
    for rows, yc in zip(subs, ycs):
        mixes.append(jnp.dot(a_ref[0, rows, :], wout_ref[0:ATTN_WIDTH, :], preferred_element_type=F32)
                     + jnp.dot(yc, wout_ref[ATTN_WIDTH:D_MODEL, :], preferred_element_type=F32))
    h1s = [_layernorm(ALPHA * h_ref[0, rows, :] + mix, g1_ref[...], b1_ref[...])
           for rows, mix in zip(subs, mixes)]
    scores = []
    for h1 in h1s:
        q = jnp.dot(h1.astype(BF16), wq_ref[...], preferred_element_type=F32) * (MEM_HEAD_DIM ** -0.5)
        qb = q.astype(BF16)
        scores.append([lax.dot_general(qb[:, sl], k_ref[0, :, sl], (((1,), (1,)), ((), ())),
                                       preferred_element_type=F32) for sl in heads])
    atts = []
    for sc in scores:
        outs = []
        for s, sl in zip(sc, heads):
            m = jnp.max(s, axis=-1, keepdims=True)
            p = jnp.exp(s - m)
            l = jnp.sum(p, axis=-1, keepdims=True)
            o = jnp.dot(p.astype(BF16), v_ref[0, :, sl], preferred_element_type=F32) / l
            outs.append(o.astype(BF16))
        atts.append(jnp.dot(jnp.concatenate(outs, axis=1), wo_ref[...], preferred_element_type=F32))
    for rows, h1, att in zip(subs, h1s, atts):
        o_ref[0, rows, :] = _layernorm(ALPHA * h1 + att, g2_ref[...], b2_ref[...])


def _mix_mem(conv, attn_n, h, ln_g, ln_b, out_g, w_out, g1, b1, wq, km, vm, wo, g2, b2, *, ts):
    b, s, d = h.shape
    nm = km.shape[1]
    tile = lambda bi, ji: (bi, ji, 0)
    const = lambda bi, ji: (0, 0)
    per_b = lambda bi, ji: (bi, 0, 0)
    vec = lambda n: pl.BlockSpec((1, n), const)
    resident = lambda: pl.BlockSpec((d, d), const, pipeline_mode=pl.Buffered(1))
    return pl.pallas_call(
        functools.partial(_mix_mem_kernel, sub_rows=SUB_ROWS),
        grid=(b, s // ts),
        in_specs=[
            pl.BlockSpec((1, ts, CONV_WIDTH), tile),
            pl.BlockSpec((1, ts, ATTN_WIDTH), tile),
            pl.BlockSpec((1, ts, d), tile),
            vec(CONV_WIDTH), vec(CONV_WIDTH), vec(CONV_WIDTH),
            resident(), vec(d), vec(d),
            resident(),
            pl.BlockSpec((1, nm, d), per_b),
            pl.BlockSpec((1, nm, d), per_b),
            resident(), vec(d), vec(d),
        ],
        out_specs=pl.BlockSpec((1, ts, d), tile),
        out_shape=jax.ShapeDtypeStruct((b, s, d), F32),
        compiler_params=pltpu.CompilerParams(
            dimension_semantics=("arbitrary", "arbitrary"), vmem_limit_bytes=VMEM_LIMIT),
        name="mix_mem",
    )(conv, attn_n, h, ln_g, ln_b, out_g, w_out, g1, b1, wq, km, vm, wo, g2, b2)


def _mlp_kernel(h_ref, w1_ref, b1_ref, w2_ref, b2_ref, g_ref, b_ref, o_ref, *, ff_chunk, sub_rows):
    subs = [slice(st * sub_rows, (st + 1) * sub_rows) for st in range(h_ref.shape[0] // sub_rows)]
    hs = [h_ref[rows, :] for rows in subs]
    hbs = [h.astype(BF16) for h in hs]
    accs = [ALPHA * h + b2_ref[...] for h in hs]
    for f in range(D_FF // ff_chunk):
        sl = slice(f * ff_chunk, (f + 1) * ff_chunk)
        acts = []
        for hb in hbs:
            a = jnp.maximum(jnp.dot(hb, w1_ref[:, sl], preferred_element_type=F32) + b1_ref[:, sl], 0.0)
            acts.append((a * a).astype(BF16))
        accs = [acc + jnp.dot(act, w2_ref[sl, :], preferred_element_type=F32)
                for acc, act in zip(accs, acts)]
    for rows, acc in zip(subs, accs):
        o_ref[rows, :] = _layernorm(acc, g_ref[...], b_ref[...])


def _mlp(h2, w1, b1, w2, b2, g3, b3, *, tm):
    m, d = h2.shape
    row = lambda i: (i, 0)
    const = lambda i: (0, 0)
    kern = functools.partial(_mlp_kernel, ff_chunk=MLP_FF_CHUNK, sub_rows=SUB_ROWS)
    return pl.pallas_call(
        kern,
        grid=(m // tm,),
        in_specs=[
            pl.BlockSpec((tm, d), row),
            pl.BlockSpec((d, D_FF), const, pipeline_mode=pl.Buffered(1)),
            pl.BlockSpec((1, D_FF), const),
            pl.BlockSpec((D_FF, d), const, pipeline_mode=pl.Buffered(1)),
            pl.BlockSpec((1, d), const),
            pl.BlockSpec((1, d), const),
            pl.BlockSpec((1, d), const),
        ],
        out_specs=pl.BlockSpec((tm, d), row),
        out_shape=jax.ShapeDtypeStruct((m, d), F32),
        compiler_params=pltpu.CompilerParams(
            dimension_semantics=("arbitrary",), vmem_limit_bytes=VMEM_LIMIT),
        name="mlp",
    )(h2, w1, b1, w2, b2, g3, b3)


def _rope_tables(seq_len, gain, scale):
    rows = seq_len // GRID_W
    row_ids = jnp.repeat(jnp.arange(rows, dtype=jnp.int32), GRID_W)
    col_ids = jnp.tile(jnp.arange(GRID_W, dtype=jnp.int32), rows)
    inv = ROPE_THETA ** (-jnp.arange(0, AXIS_DIM, 2, dtype=jnp.float32) / AXIS_DIM)
    ang = jnp.concatenate([row_ids[:, None].astype(jnp.float32) * inv,
                           col_ids[:, None].astype(jnp.float32) * inv], axis=-1)
    cos = jnp.repeat(jnp.cos(ang), 2, axis=-1)
    sin = jnp.repeat(jnp.sin(ang), 2, axis=-1)
    sign = jnp.where(jnp.arange(HEAD_DIM) % 2 == 0, -1.0, 1.0).astype(jnp.float32)
    gain_sw = gain.reshape(HEAD_DIM // 2, 2)[:, ::-1].reshape(HEAD_DIM)
    c = cos * gain * scale
    s = sin * sign * gain_sw * scale
    return jnp.tile(c, (1, LANES // HEAD_DIM)), jnp.tile(s, (1, LANES // HEAD_DIM))


def kernel(x, mem, ln_in_g, ln_in_b, w_in, q_norm_g, k_norm_g, conv_w, conv_b, conv_ln_g, conv_ln_b,
           attn_out_g, conv_out_g, w_out, ln1_g, ln1_b, w_mem_q, w_mem_kv, w_mem_o, ln2_g, ln2_b,
           w_ff1, b_ff1, w_ff2, b_ff2, ln3_g, ln3_b):
    b, s, d = x.shape
    assert d == D_MODEL and w_in.shape[0] == DEPTH == 1
    assert s % GRID_W == 0
    m = b * s
    nm = mem.shape[1]
    row = lambda a: a.reshape(1, -1)

    cq, sq = _rope_tables(s, q_norm_g[0], HEAD_DIM ** -0.5 * LOG2_E)
    ck, sk = _rope_tables(s, k_norm_g[0], 1.0)
    seg = jnp.arange(LANES) // HEAD_DIM
    bd = (seg[:, None] == seg[None, :]).astype(BF16)

    h, q, kd, vd, u = _in_proj(x.reshape(m, d), row(ln_in_g), row(ln_in_b), w_in[0].astype(BF16),
                               cq, sq, ck, sk, bd, seq=s, tm=IN_PROJ_ROWS)
    km, vm = _mem_kv(mem.reshape(b * nm, d), w_mem_kv[0], tm=MEM_KV_ROWS)

    gain_max = jnp.max(jnp.abs(jnp.stack([q_norm_g[0], k_norm_g[0]])), axis=1)
    score_bound = HEAD_DIM * gain_max[0] * gain_max[1] * (HEAD_DIM ** -0.5 * LOG2_E)
    attn_n, conv, (w_out_b, w_mq_b, w_mo_b, w_ff1_b, w_ff2_b) = lax.cond(
        score_bound <= SAFE_EXP2_BOUND,
        functools.partial(_attention, tq=ATTN_Q_ROWS, shift_by_max=False),
        functools.partial(_attention, tq=ATTN_Q_ROWS, shift_by_max=True),
        q.reshape(b, s, ATTN_WIDTH), kd.reshape(b, s, 2 * LANES), vd.reshape(b, s, 4 * LANES),
        row(attn_out_g[0]), u.reshape(b, s, CONV_WIDTH), conv_w[0], row(conv_b[0]),
        (w_out[0], w_mem_q[0], w_mem_o[0], w_ff1[0], w_ff2[0]))

    h2 = _mix_mem(conv, attn_n, h.reshape(b, s, d), row(conv_ln_g[0]), row(conv_ln_b[0]),
                  row(conv_out_g[0]), w_out_b, row(ln1_g[0]), row(ln1_b[0]),
                  w_mq_b, km.reshape(b, nm, d), vm.reshape(b, nm, d), w_mo_b,
                  row(ln2_g[0]), row(ln2_b[0]), ts=MIX_MEM_ROWS)

    out = _mlp(h2.reshape(m, d), w_ff1_b, row(b_ff1[0]), w_ff2_b, row(b_ff2[0]),
               row(ln3_g[0]), row(ln3_b[0]), tm=MLP_ROWS)
    return out.reshape(b, s, d)
```
